```python
import math
import jax
import jax.numpy as jnp
from jax import lax
import numpy as np

D_MODEL = 1024
BATCH = 16
SEQ = 2048
DEPTH = 4

GRID_W = 64
CTX_LEN = 256
N_EVEN = (DEPTH + 1) // 2
N_ODD = DEPTH // 2
N_VRES = max(N_ODD - 1, 0)

EPS = 1e-6
MLA_HEADS = 8
MLA_Q_RANK = 256
MLA_KV_RANK = 256
MLA_NOPE = 64
MLA_ROPE = 32
MLA_QK = MLA_NOPE + MLA_ROPE
MLA_V = 64
AXIS_DIM = MLA_ROPE // 2
ROPE_THETA = 10000.0
Q_BLOCK = 128
CONV_CH = 512
CONV_WIDTH = 31
EVEN_SPLITS = [MLA_Q_RANK, MLA_Q_RANK + MLA_KV_RANK, MLA_Q_RANK + MLA_KV_RANK + MLA_ROPE]
EVEN_IN = MLA_Q_RANK + MLA_KV_RANK + MLA_ROPE + 2 * CONV_CH
EVEN_MIX = MLA_HEADS * MLA_V + CONV_CH
RW_HEAD = 64
RW_HEADS = D_MODEL // RW_HEAD
RW_DECAY_LORA = 64
RW_A_LORA = 64
RW_V_LORA = 32
RW_G_LORA = 160
RW_GN_EPS = 6.4e-4
RW_BRANCHES = 6
FFN_HIDDEN = -(-8 * D_MODEL // (3 * 256)) * 256

kernel_name = 'hybrid_mla_conformer_rwkv7_dit'


def rms_norm(x, g):
    xf = x.astype(jnp.float32)
    y = xf * lax.rsqrt(jnp.mean(xf * xf, axis=-1, keepdims=True) + EPS)
    return (y * g.astype(jnp.float32)).astype(x.dtype)


def layer_norm(x, g, b, eps=1e-5):
    xf = x.astype(jnp.float32)
    mu = jnp.mean(xf, axis=-1, keepdims=True)
    var = jnp.mean(jnp.square(xf - mu), axis=-1, keepdims=True)
    y = (xf - mu) * lax.rsqrt(var + eps)
    return (y * g.astype(jnp.float32) + b.astype(jnp.float32)).astype(x.dtype)


def ada_modulation(cond, w, b):
    m = jax.nn.silu(cond) @ w + b
    return [t[:, None, :] for t in jnp.split(m, 6, axis=-1)]


def swiglu(h, w1, w3, w2):
    return (jax.nn.silu(h @ w1) * (h @ w3)) @ w2


def axial_rope_tables(n):
    rows = n // GRID_W
    row = jnp.repeat(jnp.arange(rows, dtype=jnp.float32), GRID_W)
    col = jnp.tile(jnp.arange(GRID_W, dtype=jnp.float32), rows)
    inv = ROPE_THETA ** (-jnp.arange(0, AXIS_DIM, 2, dtype=jnp.float32) / AXIS_DIM)
    ang_r = row[:, None] * inv
    ang_c = col[:, None] * inv
    return (jnp.cos(ang_r), jnp.sin(ang_r), jnp.cos(ang_c), jnp.sin(ang_c))


def rotate(x, cos, sin):
    m = x.shape[-1] // 2
    x1, x2 = x[..., :m], x[..., m:]
    cos = cos[:, None, :].astype(x.dtype)
    sin = sin[:, None, :].astype(x.dtype)
    return jnp.concatenate([x1 * cos - x2 * sin, x2 * cos + x1 * sin], axis=-1)


def rope_latent(t, tabs):
    cr, sr, cc, sc = tabs
    t_nope = t[..., :MLA_NOPE]
    t_row = t[..., MLA_NOPE:MLA_NOPE + AXIS_DIM]
    t_col = t[..., MLA_NOPE + AXIS_DIM:]
    return jnp.concatenate([t_nope, rotate(t_row, cr, sr), rotate(t_col, cc, sc)], axis=-1)


def block_attention(q, k, v):
    b, lq, h, dq = q.shape
    dv = v.shape[-1]
    nb = lq // Q_BLOCK
    qb = jnp.moveaxis(q.reshape(b, nb, Q_BLOCK, h, dq), 1, 0)
    scale = dq ** -0.5

    def one_block(qblk):
        s = jnp.einsum('bqhd,bkhd->bhqk', qblk, k).astype(jnp.float32) * scale
        pr = jax.nn.softmax(s, axis=-1).astype(v.dtype)
        return jnp.einsum('bhqk,bkhd->bqhd', pr, v)

    o = lax.map(one_block, qb)
    return jnp.moveaxis(o, 0, 1).reshape(b, lq, h * dv)


def conformer_conv(u, p):
    a, gate = jnp.split(u, 2, axis=-1)
    y = a * jax.nn.sigmoid(gate)
    y = lax.conv_general_dilated(
        y, p['conv_w'][:, None, :], window_strides=(1,),
        padding=[(CONV_WIDTH // 2, CONV_WIDTH // 2)],
        dimension_numbers=('NWC', 'WIO', 'NWC'),
        feature_group_count=CONV_CH) + p['conv_b']
    y = layer_norm(y, p['conv_ln_g'], p['conv_ln_b'])
    return jax.nn.silu(y)


def even_mixer(a_ctx, a_lat, p, rope, need_ctx_out):
    def mla_q(zq):
        b, n = zq.shape[:2]
        q = (rms_norm(zq, p['q_norm']) @ p['wq_b']).reshape(b, n, MLA_HEADS, MLA_QK)
        return rms_norm(q, p['q_qk'])

    def mla_kv(zkv, zr):
        b, n = zkv.shape[:2]
        kv = (rms_norm(zkv, p['kv_norm']) @ p['wkv_b']).reshape(b, n, MLA_HEADS, MLA_NOPE + MLA_V)
        k_rope = jnp.broadcast_to(zr[:, :, None, :], (b, n, MLA_HEADS, MLA_ROPE))
        k = rms_norm(jnp.concatenate([kv[..., :MLA_NOPE], k_rope], axis=-1), p['k_qk'])
        return k, kv[..., MLA_NOPE:]

    def merge_out(o_attn, zconv):
        return jnp.concatenate([o_attn, conformer_conv(zconv, p)], axis=-1) @ p['w_out']

    zq_c, zkv_c, zr_c, zconv_c = jnp.split(a_ctx @ p['w_in'], EVEN_SPLITS, axis=-1)
    zq_l, zkv_l, zr_l, zconv_l = jnp.split(a_lat @ p['w_in'], EVEN_SPLITS, axis=-1)
    k_c, v_c = mla_kv(zkv_c, zr_c)
    k_l, v_l = mla_kv(zkv_l, zr_l)
    k_l = rope_latent(k_l, rope)
    q_l = rope_latent(mla_q(zq_l), rope)
    o_l = block_attention(q_l, jnp.concatenate([k_c, k_l], axis=1), jnp.concatenate([v_c, v_l], axis=1))
    out_lat = merge_out(o_l, zconv_l)
    out_ctx = merge_out(block_attention(mla_q(zq_c), k_c, v_c), zconv_c) if need_ctx_out else None
    return out_ctx, out_lat


def rwkv_features(x, p, v_first, need_out):
    b, n, _ = x.shape
    heads = lambda t: t.reshape(b, n, RW_HEADS, RW_HEAD)
    zeros = jnp.zeros_like(x[:, :1])
    d_prev = jnp.concatenate([zeros, x[:, :-1]], axis=1) - x
    d_next = jnp.concatenate([x[:, 1:], zeros], axis=1) - x
    mix = lambda i: x + d_prev * p['mu'][0, i] + d_next * p['mu'][1, i]
    xw, xk, xv, xa = mix(1), mix(2), mix(3), mix(4)
    k = xk @ p['wk']
    v = xv @ p['wv']
    if v_first is not None:
        v = v + (v_first - v) * jax.nn.sigmoid(p['v0'] + (xv @ p['v1']) @ p['v2'])
    kkf = heads(k * p['k_k']).astype(jnp.float32)
    kk = kkf / jnp.maximum(jnp.sqrt(jnp.sum(kkf * kkf, axis=-1, keepdims=True)), 1e-12)
    dirs = []
    for d in range(2):
        w_log = -jax.nn.softplus(-(p['w0'][d] + jnp.tanh(xw @ p['w1'][d]) @ p['w2'][d])) - 0.5
        decay = jnp.exp(-jnp.exp(w_log.astype(jnp.float32)))
        a_rate = jax.nn.sigmoid(p['a0'][d] + (xa @ p['a1'][d]) @ p['a2'][d])
        k_d = k * (1 + (a_rate - 1) * p['k_a'])
        dirs.append((heads(decay), heads(k_d), kk * heads(a_rate).astype(jnp.float32)))
    feats = {'v': v, 'vh': heads(v), 'kk': kk, 'dirs': dirs, 'r': None, 'g': None}
    if need_out:
        feats['r'] = heads(mix(0) @ p['wr'])
        feats['g'] = jax.nn.sigmoid(mix(5) @ p['g1']) @ p['g2']
    return feats


def wkv_scan(s0, decay, k, v, kk, b, r, reverse):
    tm = lambda t: jnp.moveaxis(t.astype(jnp.float32), 1, 0)
    emit = r is not None
    xs = (tm(decay), tm(k), tm(v), tm(kk), tm(b)) + ((tm(r),) if emit else ())

    def step(S, inp):
        w_t, k_t, v_t, kk_t, b_t = inp[:5]
        sa = jnp.einsum('bhvk,bhk->bhv', S, kk_t)
        S = S * w_t[:, :, None, :] - sa[..., None] * b_t[:, :, None, :] + v_t[..., None] * k_t[:, :, None, :]
        y = jnp.einsum('bhvk,bhk->bhv', S, inp[5]) if emit else None
        return S, y

    S, ys = lax.scan(step, s0, xs, reverse=reverse)
    return S, (jnp.moveaxis(ys, 0, 1) if emit else None)


def rwkv_readout(y, f, p):
    b, n = y.shape[:2]
    mu = jnp.mean(y, axis=-1, keepdims=True)
    var = jnp.mean(jnp.square(y - mu), axis=-1, keepdims=True)
    yn = ((y - mu) * lax.rsqrt(var + RW_GN_EPS)).reshape(b, n, D_MODEL)
    yn = yn * p['ln_g'].astype(jnp.float32) + p['ln_b'].astype(jnp.float32)
    rf = f['r'].astype(jnp.float32)
    rk = p['r_k'].astype(jnp.float32)
    kmix = f['dirs'][0][1].astype(jnp.float32) * rk[0] + f['dirs'][1][1].astype(jnp.float32) * rk[1]
    coef = jnp.sum(rf * kmix, axis=-1, keepdims=True)
    bonus = (coef * f['vh'].astype(jnp.float32)).reshape(b, n, D_MODEL)
    out = ((yn + bonus) * f['g'].astype(jnp.float32)).astype(f['g'].dtype)
    return out @ p['wo']


def odd_mixer(a_ctx, a_lat, p, v_first, need_ctx_out):
    vf_ctx, vf_lat = v_first if v_first is not None else (None, None)
    fc = rwkv_features(a_ctx, p, vf_ctx, need_ctx_out)
    fl = rwkv_features(a_lat, p, vf_lat, True)
    s0 = jnp.zeros((a_lat.shape[0], RW_HEADS, RW_HEAD, RW_HEAD), jnp.float32)
    y_ctx, y_lat = [], []
    for d in range(2):
        rev = d == 1
        dec_c, k_c, b_c = fc['dirs'][d]
        dec_l, k_l, b_l = fl['dirs'][d]
        s_ctx, yc = wkv_scan(s0, dec_c, k_c, fc['vh'], fc['kk'], b_c, fc['r'], rev)
        _, yl = wkv_scan(s_ctx, dec_l, k_l, fl['vh'], fl['kk'], b_l, fl['r'], rev)
        y_ctx.append(yc)
        y_lat.append(yl)
    out_lat = rwkv_readout(y_lat[0] + y_lat[1], fl, p)
    out_ctx = rwkv_readout(y_ctx[0] + y_ctx[1], fc, p) if need_ctx_out else None
    new_vf = v_first if v_first is not None else (fc['v'], fl['v'])
    return out_ctx, out_lat, new_vf


def setup_inputs(seed: int = 0) -> dict:
    key = jax.random.key(seed)
    ks = iter(jax.random.split(key, 64))
    nrm = lambda shape, scale: jax.random.normal(next(ks), shape, jnp.float32) * scale
    gain = lambda shape: 1.0 + nrm(shape, 0.02)
    uni = lambda shape, lo, hi: jax.random.uniform(next(ks), shape, jnp.float32, lo, hi)
    D = D_MODEL
    return {
        'x': nrm((BATCH, SEQ, D), 1.0),
        'c': nrm((BATCH, D), 1.0),
        'ctx': nrm((BATCH, CTX_LEN, D), 1.0),
        'c_ctx': nrm((D,), 1.0),
        'ada_w': nrm((DEPTH, D, 6 * D), 0.5 * D ** -0.5),
        'ada_b': nrm((DEPTH, 6 * D), 0.02),
        'norm_mix': gain((DEPTH, D)),
        'norm_ffn': gain((DEPTH, D)),
        'ffn_w1': nrm((DEPTH, D, FFN_HIDDEN), D ** -0.5),
        'ffn_w3': nrm((DEPTH, D, FFN_HIDDEN), D ** -0.5),
        'ffn_w2': nrm((DEPTH, FFN_HIDDEN, D), FFN_HIDDEN ** -0.5),
        'even_w_in': nrm((N_EVEN, D, EVEN_IN), D ** -0.5),
        'mla_q_norm': gain((N_EVEN, MLA_Q_RANK)),
        'mla_wq_b': nrm((N_EVEN, MLA_Q_RANK, MLA_HEADS * MLA_QK), MLA_Q_RANK ** -0.5),
        'mla_kv_norm': gain((N_EVEN, MLA_KV_RANK)),
        'mla_wkv_b': nrm((N_EVEN, MLA_KV_RANK, MLA_HEADS * (MLA_NOPE + MLA_V)), MLA_KV_RANK ** -0.5),
        'mla_q_qk': gain((N_EVEN, MLA_QK)),
        'mla_k_qk': gain((N_EVEN, MLA_QK)),
        'conv_w': nrm((N_EVEN, CONV_WIDTH, CONV_CH), CONV_WIDTH ** -0.5),
        'conv_b': nrm((N_EVEN, CONV_CH), 0.02),
        'conv_ln_g': gain((N_EVEN, CONV_CH)),
        'conv_ln_b': nrm((N_EVEN, CONV_CH), 0.02),
        'even_w_out': nrm((N_EVEN, EVEN_MIX, D), EVEN_MIX ** -0.5),
        'rw_mu': uni((N_ODD, 2, RW_BRANCHES, D), 0.0, 0.5),
        'rw_wr': nrm((N_ODD, D, D), D ** -0.5),
        'rw_wk': nrm((N_ODD, D, D), D ** -0.5),
        'rw_wv': nrm((N_ODD, D, D), D ** -0.5),
        'rw_w0': uni((N_ODD, 2, D), -6.0, -1.0),
        'rw_w1': nrm((N_ODD, 2, D, RW_DECAY_LORA), D ** -0.5),
        'rw_w2': nrm((N_ODD, 2, RW_DECAY_LORA, D), 0.5 * RW_DECAY_LORA ** -0.5),
        'rw_a0': nrm((N_ODD, 2, D), 0.5),
        'rw_a1': nrm((N_ODD, 2, D, RW_A_LORA), D ** -0.5),
        'rw_a2': nrm((N_ODD, 2, RW_A_LORA, D), 0.5 * RW_A_LORA ** -0.5),
        'rw_v0': nrm((N_VRES, D), 0.5),
        'rw_v1': nrm((N_VRES, D, RW_V_LORA), D ** -0.5),
        'rw_v2': nrm((N_VRES, RW_V_LORA, D), 0.5 * RW_V_LORA ** -0.5),
        'rw_k_k': 1.0 + nrm((N_ODD, D), 0.1),
        'rw_k_a': 1.0 + nrm((N_ODD, D), 0.1),
        'rw_r_k': nrm((N_ODD, 2, RW_HEADS, RW_HEAD), 0.1),
        'rw_g1': nrm((N_ODD, D, RW_G_LORA), D ** -0.5),
        'rw_g2': nrm((N_ODD, RW_G_LORA, D), RW_G_LORA ** -0.5),
        'rw_ln_g': gain((N_ODD, D)),
        'rw_ln_b': nrm((N_ODD, D), 0.02),
        'rw_wo': nrm((N_ODD, D, D), D ** -0.5),
    }


def reference(x, c, ctx, c_ctx, ada_w, ada_b, norm_mix, norm_ffn, ffn_w1, ffn_w3, ffn_w2,
              even_w_in, mla_q_norm, mla_wq_b, mla_kv_norm, mla_wkv_b, mla_q_qk, mla_k_qk,
              conv_w, conv_b, conv_ln_g, conv_ln_b, even_w_out,
              rw_mu, rw_wr, rw_wk, rw_wv, rw_w0, rw_w1, rw_w2, rw_a0, rw_a1, rw_a2,
              rw_v0, rw_v1, rw_v2, rw_k_k, rw_k_a, rw_r_k, rw_g1, rw_g2, rw_ln_g, rw_ln_b, rw_wo):
    rope = axial_rope_tables(x.shape[1])
    h_ctx, h_lat = ctx, x
    v_first = None
    for i in range(DEPTH):
        last = i == DEPTH - 1
        j = i // 2
        sh1_l, sc1_l, g1_l, sh2_l, sc2_l, g2_l = ada_modulation(c, ada_w[i], ada_b[i])
        sh1_c, sc1_c, g1_c, sh2_c, sc2_c, g2_c = ada_modulation(c_ctx[None, :], ada_w[i], ada_b[i])
        a_lat = rms_norm(h_lat, norm_mix[i]) * (1 + sc1_l) + sh1_l
        a_ctx = rms_norm(h_ctx, norm_mix[i]) * (1 + sc1_c) + sh1_c
        if i % 2 == 0:
            p = {'w_in': even_w_in[j], 'q_norm': mla_q_norm[j], 'wq_b': mla_wq_b[j],
                 'kv_norm': mla_kv_norm[j], 'wkv_b': mla_wkv_b[j], 'q_qk': mla_q_qk[j],
                 'k_qk': mla_k_qk[j], 'conv_w': conv_w[j], 'conv_b': conv_b[j],
                 'conv_ln_g': conv_ln_g[j], 'conv_ln_b': conv_ln_b[j], 'w_out': even_w_out[j]}
            o_ctx, o_lat = even_mixer(a_ctx, a_lat, p, rope, not last)
        else:
            p = {'mu': rw_mu[j], 'wr': rw_wr[j], 'wk': rw_wk[j], 'wv': rw_wv[j],
                 'w0': rw_w0[j], 'w1': rw_w1[j], 'w2': rw_w2[j],
                 'a0': rw_a0[j], 'a1': rw_a1[j], 'a2': rw_a2[j],
                 'k_k': rw_k_k[j], 'k_a': rw_k_a[j], 'r_k': rw_r_k[j],
                 'g1': rw_g1[j], 'g2': rw_g2[j], 'ln_g': rw_ln_g[j], 'ln_b': rw_ln_b[j], 'wo': rw_wo[j]}
            if j > 0:
                p['v0'] = rw_v0[j - 1]
                p['v1'] = rw_v1[j - 1]
                p['v2'] = rw_v2[j - 1]
            o_ctx, o_lat, v_first = odd_mixer(a_ctx, a_lat, p, v_first, not last)
        h_lat = h_lat + g1_l * o_lat
        f_lat = rms_norm(h_lat, norm_ffn[i]) * (1 + sc2_l) + sh2_l
        h_lat = h_lat + g2_l * swiglu(f_lat, ffn_w1[i], ffn_w3[i], ffn_w2[i])
        if not last:
            h_ctx = h_ctx + g1_c * o_ctx
            f_ctx = rms_norm(h_ctx, norm_ffn[i]) * (1 + sc2_c) + sh2_c
            h_ctx = h_ctx + g2_c * swiglu(f_ctx, ffn_w1[i], ffn_w3[i], ffn_w2[i])
    return h_lat
```

```python
import functools
import math

import jax
import jax.numpy as jnp
from jax import lax
from jax.experimental import pallas as pl
from jax.experimental.pallas import tpu as pltpu

F32 = jnp.float32
BF16 = jnp.bfloat16

EPS = 1e-6
GRID_W = 64
ROPE_THETA = 10000.0
MLA_HEADS = 8
MLA_NOPE = 64
MLA_ROPE = 32
MLA_QK = MLA_NOPE + MLA_ROPE
MLA_V = 64
AXIS_DIM = MLA_ROPE // 2
CONV_WIDTH = 31
RW_HEAD = 64
RW_GN_EPS = 6.4e-4

LANES = 128
VMEM_LIMIT = 48 * 1024 * 1024

SCAN_C = 64
SCAN_G = 4
SCAN_W = SCAN_G * RW_HEAD

NT_DIMS = (((1,), (1,)), ((), ()))
TN_DIMS = (((0,), (0,)), ((), ()))


def _pick(n, cands):
    for c in cands:
        if c <= n and n % c == 0:
            return c
    return n


def _mm_kernel(x_ref, w_ref, o_ref, *, pre):
    x = x_ref[...]
    if pre == "silu":
        x = x * jax.nn.sigmoid(x)
    elif pre == "tanh":
        x = jnp.tanh(x)
    elif pre == "sigmoid":
        x = jax.nn.sigmoid(x)
    o_ref[...] = jnp.dot(x.astype(BF16), w_ref[...].astype(BF16), preferred_element_type=F32)


def mm(x, w, pre=None):
    m, k = x.shape
    n = w.shape[1]
    tm = _pick(m, (512, 384, 256, 128, 64, 32, 16, 8))
    tn = n if n <= 1024 else _pick(n, (1024, 896, 768, 640, 512, 384, 256, 128))
    return pl.pallas_call(
        functools.partial(_mm_kernel, pre=pre),
        grid=(m // tm, n // tn),
        in_specs=[pl.BlockSpec((tm, k), lambda i, j: (i, 0)),
                  pl.BlockSpec((k, tn), lambda i, j: (0, j))],
        out_specs=pl.BlockSpec((tm, tn), lambda i, j: (i, j)),
        out_shape=jax.ShapeDtypeStruct((m, n), F32),
        compiler_params=pltpu.CompilerParams(
            dimension_semantics=("parallel", "parallel"), vmem_limit_bytes=VMEM_LIMIT),
        name="mm",
    )(x, w)


def _swiglu_kernel(x_ref, w1_ref, w3_ref, o_ref):
    x = x_ref[...].astype(BF16)
    a = jnp.dot(x, w1_ref[...], preferred_element_type=F32)
    b = jnp.dot(x, w3_ref[...], preferred_element_type=F32)
    o_ref[...] = a * jax.nn.sigmoid(a) * b


def mm_swiglu(x, w1, w3):
    m, k = x.shape
    n = w1.shape[1]
    tm = _pick(m, (512, 384, 256, 128, 64, 32, 16, 8))
    tn = _pick(n, (1408, 1024, 768, 512, 384, 256, 128))
    return pl.pallas_call(
        _swiglu_kernel,
        grid=(m // tm, n // tn),
        in_specs=[pl.BlockSpec((tm, k), lambda i, j: (i, 0)),
                  pl.BlockSpec((k, tn), lambda i, j: (0, j)),
                  pl.BlockSpec((k, tn), lambda i, j: (0, j))],
        out_specs=pl.BlockSpec((tm, tn), lambda i, j: (i, j)),
        out_shape=jax.ShapeDtypeStruct((m, n), F32),
        compiler_params=pltpu.CompilerParams(
            dimension_semantics=("parallel", "parallel"), vmem_limit_bytes=VMEM_LIMIT),
        name="mm_swiglu",
    )(x, w1, w3)


def _attn_kernel(q_ref, k_ref, v_ref, o_ref, *, scale):
    q = q_ref[0, 0].astype(BF16)
    k = k_ref[0, 0].astype(BF16)
    v = v_ref[0, 0].astype(BF16)
    s = lax.dot_general(q, k, NT_DIMS, preferred_element_type=F32) * scale
    m = jnp.max(s, axis=-1, keepdims=True)
    p = jnp.exp(s - m)
    l = jnp.sum(p, axis=-1, keepdims=True)
    o = jnp.dot(p.astype(BF16), v, preferred_element_type=F32)
    o_ref[0, 0] = o / l


def attention(q, k, v, q_start, q_len, k_len, scale):
    b, h, _, dp = q.shape
    tq = _pick(math.gcd(q_len, q_start), (256, 128, 64, 32, 16, 8))
    q0 = q_start // tq
    return pl.pallas_call(
        functools.partial(_attn_kernel, scale=scale),
        grid=(b, h, q_len // tq),
        in_specs=[pl.BlockSpec((1, 1, tq, dp), lambda bi, hi, i: (bi, hi, i + q0, 0)),
                  pl.BlockSpec((1, 1, k_len, dp), lambda bi, hi, i: (bi, hi, 0, 0)),
                  pl.BlockSpec((1, 1, k_len, dp), lambda bi, hi, i: (bi, hi, 0, 0))],
        out_specs=pl.BlockSpec((1, 1, tq, dp), lambda bi, hi, i: (bi, hi, i, 0)),
        out_shape=jax.ShapeDtypeStruct((b, h, q_len, dp), F32),
        compiler_params=pltpu.CompilerParams(
            dimension_semantics=("parallel", "parallel", "parallel"), vmem_limit_bytes=VMEM_LIMIT),
        name="attention",
    )(q, k, v)


CONV_ROWS = 64
CONV_HALO = 16


def _conv_kernel(u_ref, w_ref, b_ref, g_ref, beta_ref, o_ref, pad_ref, y_ref, *, segments, ch):
    half = CONV_WIDTH // 2
    win = CONV_ROWS + 2 * CONV_HALO
    nlb = ch // LANES
    for seg_start, seg_len in segments:
        zeros = jnp.zeros((CONV_HALO, ch), F32)
        pad_ref[0:CONV_HALO, :] = zeros
        pad_ref[CONV_HALO + seg_len:2 * CONV_HALO + seg_len, :] = zeros

        def glu_body(i, carry):
            r0 = pl.multiple_of(i * CONV_ROWS, CONV_ROWS)
            u = u_ref[0, pl.ds(seg_start + r0, CONV_ROWS), :]
            pad_ref[pl.ds(CONV_HALO + r0, CONV_ROWS), :] = u[:, :ch] * jax.nn.sigmoid(u[:, ch:])
            return carry

        lax.fori_loop(0, seg_len // CONV_ROWS, glu_body, 0)

        def conv_body(i, carry):
            r0 = pl.multiple_of(i * CONV_ROWS, CONV_ROWS)
            for lb in range(nlb):
                ls = slice(lb * LANES, (lb + 1) * LANES)
                xwin = pad_ref[pl.ds(r0, win), ls]
                acc = jnp.zeros((CONV_ROWS, LANES), F32)
                for j in range(CONV_WIDTH):
                    off = CONV_HALO - half + j
                    shifted = pltpu.roll(xwin, shift=(win - off) % win, axis=0)[0:CONV_ROWS]
                    acc = acc + shifted * w_ref[j:j + 1, ls]
                y_ref[pl.ds(seg_start + r0, CONV_ROWS), ls] = acc + b_ref[0:1, ls]
            return carry

        lax.fori_loop(0, seg_len // CONV_ROWS, conv_body, 0)

    def ln_body(i, carry):
        r0 = pl.multiple_of(i * CONV_ROWS, CONV_ROWS)
        y = y_ref[pl.ds(r0, CONV_ROWS), :]
        mu = jnp.mean(y, axis=-1, keepdims=True)
        d = y - mu
        var = jnp.mean(d * d, axis=-1, keepdims=True)
        z = d * lax.rsqrt(var + 1e-5) * g_ref[0:1, :] + beta_ref[0:1, :]
        o_ref[0, pl.ds(r0, CONV_ROWS), :] = z * jax.nn.sigmoid(z)
        return carry

    total = sum(s[1] for s in segments)
    lax.fori_loop(0, total // CONV_ROWS, ln_body, 0)


def conformer_conv(u, conv_w, conv_b, ln_g, ln_b, segments):
    b, l, ch2 = u.shape
    ch = ch2 // 2
    max_seg = max(s[1] for s in segments)
    row = lambda t: t.reshape(1, ch)
    return pl.pallas_call(
        functools.partial(_conv_kernel, segments=segments, ch=ch),
        grid=(b,),
        in_specs=[pl.BlockSpec((1, l, ch2), lambda i: (i, 0, 0)),
                  pl.BlockSpec((CONV_WIDTH, ch), lambda i: (0, 0)),
                  pl.BlockSpec((1, ch), lambda i: (0, 0)),
                  pl.BlockSpec((1, ch), lambda i: (0, 0)),
                  pl.BlockSpec((1, ch), lambda i: (0, 0))],
        out_specs=pl.BlockSpec((1, l, ch), lambda i: (i, 0, 0)),
        out_shape=jax.ShapeDtypeStruct((b, l, ch), F32),
        scratch_shapes=[pltpu.VMEM((max_seg + 2 * CONV_HALO, ch), F32),
                        pltpu.VMEM((l, ch), F32)],
        compiler_params=pltpu.CompilerParams(
            dimension_semantics=("parallel",), vmem_limit_bytes=VMEM_LIMIT),
        name="conformer_conv",
    )(u, conv_w, row(conv_b), row(ln_g), row(ln_b))


def _bdot(a, b, dims=None):
    a = a.astype(BF16)
    b = b.astype(BF16)
    if dims is None:
        return jnp.dot(a, b, preferred_element_type=F32)
    return lax.dot_general(a, b, dims, preferred_element_type=F32)


def _split(x):
    hi = x.astype(BF16)
    return hi, (x - hi.astype(F32)).astype(BF16)


def _pdot(a, b, dims=None, passes=1):
    if passes == 1:
        return _bdot(a, b, dims)
    if passes == "2r":
        b_hi, b_lo = _split(b)
        return _bdot(a, b_hi, dims) + _bdot(a, b_lo, dims)
    if passes == "2l":
        a_hi, a_lo = _split(a)
        return _bdot(a_hi, b, dims) + _bdot(a_lo, b, dims)
    a_hi, a_lo = _split(a)
    b_hi, b_lo = _split(b)
    return _bdot(a_hi, b_hi, dims) + (_bdot(a_hi, b_lo, dims) + _bdot(a_lo, b_hi, dims))


SCAN_PASSES = {"g": 1, "sh": 1, "ak": 1, "apply": 1, "square": 1, "y": 1, "upd": 1}


def _tile_rows(x, mask):
    return jnp.concatenate([x] * SCAN_G, axis=0) * mask


def _scan_kernel(lw_ref, kd_ref, bd_ref, r_ref, v_ref, kk_ref, cum_ref, strict_ref, incl_ref, lvl_ref,
                 blk_ref, y_ref, state_ref):
    c = SCAN_C
    @pl.when(pl.program_id(2) == 0)
    def _():
        state_ref[...] = jnp.zeros_like(state_ref)

    cum = cum_ref[0]
    strict = strict_ref[0]
    incl = incl_ref[0]
    blk = blk_ref[...]
    ngroups = lw_ref.shape[-1] // SCAN_W
    for g in range(ngroups):
        ls = slice(g * SCAN_W, (g + 1) * SCAN_W)
        lw = lw_ref[0, 0, :, ls]
        kd = kd_ref[0, 0, :, ls]
        bd = bd_ref[0, 0, :, ls]
        r = r_ref[0, :, ls]
        v = v_ref[0, :, ls]
        kk = kk_ref[0, :, ls]
        st = state_ref[g]

        lw_hi = lw.astype(BF16).astype(F32)
        lp = _bdot(cum, lw_hi) + _bdot(cum, lw - lw_hi)
        total = jnp.sum(lw, axis=0, keepdims=True)
        p_incl = jnp.exp(lp)
        inv = jnp.exp(-lp)
        tail = jnp.exp(total - lp)
        at = -kk * jnp.exp(lp - lw)
        rt = r * p_incl
        ar = jnp.concatenate([at, rt], axis=0)
        ps = SCAN_PASSES
        g1 = _pdot(ar, _tile_rows(bd * inv, blk), NT_DIMS, ps["g"])
        g2 = _pdot(ar, _tile_rows(kd * inv, blk), NT_DIMS, ps["g"])
        sh = _pdot(ar, st, NT_DIMS, ps["sh"])
        n = g1[:c] * strict
        ak = g2[:c] * strict
        rb = g1[c:] * incl
        rk = g2[c:] * incl
        vext = _tile_rows(v, blk)
        rhs = sh[:c] + _pdot(ak, vext, None, ps["ak"])
        tmat = (incl - strict) + n * lvl_ref[0, 0]
        for lv in range(1, lvl_ref.shape[1]):
            m1 = _pdot(n * lvl_ref[0, lv], _tile_rows(tmat, blk), None, ps["square"])
            tmat = tmat + _pdot(tmat, _tile_rows(m1, blk), None, ps["square"])
        u = _pdot(tmat, _tile_rows(rhs, blk), None, ps["apply"])
        y = sh[c:] + _pdot(jnp.concatenate([rb, rk], axis=1),
                           jnp.concatenate([_tile_rows(u, blk), vext], axis=0), None, ps["y"])
        y_ref[0, 0, :, ls] = y
        upd = _pdot(jnp.concatenate([u, v], axis=0),
                    jnp.concatenate([bd * tail, kd * tail], axis=0), TN_DIMS, ps["upd"])
        state_ref[g] = (st * jnp.exp(total) + upd) * blk


def _scan_masks():
    c, g = SCAN_C, SCAN_G
    t = jnp.arange(c)[:, None]
    s = jnp.arange(c)[None, :]
    cum = jnp.stack([s <= t, s >= t]).astype(F32)
    sg = jnp.arange(g * c)[None, :] % c
    strict = jnp.stack([sg < t, sg > t]).astype(F32)
    incl = jnp.stack([sg <= t, sg >= t]).astype(F32)
    sizes = [1 << i for i in range(c.bit_length() - 1)]
    lvl = jnp.stack([((t // (2 * s) == sg // (2 * s)) & (t // s != sg // s)).astype(F32) for s in sizes])
    lvl = strict[:, None] * lvl[None]
    rb = jnp.arange(SCAN_W)[:, None] // RW_HEAD
    cb = jnp.arange(SCAN_W)[None, :] // RW_HEAD
    blk = (rb == cb).astype(F32)
    return cum, strict, incl, lvl, blk


def wkv_scan(lw, kd, bd, r, v, kk, lc):
    _, b, l, d = lw.shape
    c = SCAN_C
    nc = l // c
    ncc = lc // c
    cum, strict, incl, lvl, blk = _scan_masks()
    nlv = lvl.shape[1]

    def chunk(di, ci):
        rev = jnp.where(ci < ncc, ncc - 1 - ci, nc - 1 + ncc - ci)
        return jnp.where(di == 0, ci, rev)

    dspec = pl.BlockSpec((1, 1, c, d), lambda bi, di, ci: (di, bi, chunk(di, ci), 0))
    sspec = pl.BlockSpec((1, c, d), lambda bi, di, ci: (bi, chunk(di, ci), 0))
    return pl.pallas_call(
        _scan_kernel,
        grid=(b, 2, nc),
        in_specs=[dspec, dspec, dspec, sspec, sspec, sspec,
                  pl.BlockSpec((1, c, c), lambda bi, di, ci: (di, 0, 0)),
                  pl.BlockSpec((1, c, SCAN_G * c), lambda bi, di, ci: (di, 0, 0)),
                  pl.BlockSpec((1, c, SCAN_G * c), lambda bi, di, ci: (di, 0, 0)),
                  pl.BlockSpec((1, nlv, c, SCAN_G * c), lambda bi, di, ci: (di, 0, 0, 0)),
                  pl.BlockSpec((SCAN_W, SCAN_W), lambda bi, di, ci: (0, 0))],
        out_specs=dspec,
        out_shape=jax.ShapeDtypeStruct((2, b, l, d), F32),
        scratch_shapes=[pltpu.VMEM((d // SCAN_W, SCAN_W, SCAN_W), F32)],
        compiler_params=pltpu.CompilerParams(
            dimension_semantics=("parallel", "parallel", "arbitrary"), vmem_limit_bytes=VMEM_LIMIT),
        name="wkv_scan",
    )(lw, kd, bd, r, v, kk, cum, strict, incl, lvl, blk)


def _rms(x, g):
    return x * lax.rsqrt(jnp.mean(x * x, axis=-1, keepdims=True) + EPS) * g


def _rope_tables(lc, ll):
    rows = ll // GRID_W
    row = jnp.repeat(jnp.arange(rows, dtype=F32), GRID_W)
    col = jnp.tile(jnp.arange(GRID_W, dtype=F32), rows)
    inv = ROPE_THETA ** (-jnp.arange(0, AXIS_DIM, 2, dtype=F32) / AXIS_DIM)
    ang_r = row[:, None] * inv
    ang_c = col[:, None] * inv
    pad = lambda t, fill: jnp.concatenate([jnp.full((lc, t.shape[1]), fill, F32), t], axis=0)
    return (pad(jnp.cos(ang_r), 1.0), pad(jnp.sin(ang_r), 0.0),
            pad(jnp.cos(ang_c), 1.0), pad(jnp.sin(ang_c), 0.0))


def _rotate(x, cos, sin):
    m = x.shape[-1] // 2
    x1, x2 = x[..., :m], x[..., m:]
    cos = cos[:, None, :]
    sin = sin[:, None, :]
    return jnp.concatenate([x1 * cos - x2 * sin, x2 * cos + x1 * sin], axis=-1)


def _rope(t, tabs):
    cr, sr, cc, sc = tabs
    return jnp.concatenate([t[..., :MLA_NOPE],
                            _rotate(t[..., MLA_NOPE:MLA_NOPE + AXIS_DIM], cr, sr),
                            _rotate(t[..., MLA_NOPE + AXIS_DIM:], cc, sc)], axis=-1)


def _even_mixer(a, p, rope, lc):
    b, l, d = a.shape
    ll = l - lc
    q_rank = p['q_norm'].shape[0]
    kv_rank = p['kv_norm'].shape[0]
    n_in = p['w_in'].shape[1]
    n_pad = -(-n_in // LANES) * LANES
    w_in = jnp.pad(p['w_in'], ((0, 0), (0, n_pad - n_in))).astype(BF16)
    z = mm(a.reshape(b * l, d), w_in).reshape(b, l, n_pad)
    zq = z[..., :q_rank]
    zkv = z[..., q_rank:q_rank + kv_rank]
    zr = z[..., q_rank + kv_rank:q_rank + kv_rank + MLA_ROPE]
    zconv = z[..., q_rank + kv_rank + MLA_ROPE:n_in]

    q = mm(_rms(zq, p['q_norm']).reshape(b * l, q_rank), p['wq_b'].astype(BF16))
    q = _rope(_rms(q.reshape(b, l, MLA_HEADS, MLA_QK), p['q_qk']), rope)
    kv = mm(_rms(zkv, p['kv_norm']).reshape(b * l, kv_rank), p['wkv_b'].astype(BF16))
    kv = kv.reshape(b, l, MLA_HEADS, MLA_NOPE + MLA_V)
    k_rope = jnp.broadcast_to(zr[:, :, None, :], (b, l, MLA_HEADS, MLA_ROPE))
    k = _rope(_rms(jnp.concatenate([kv[..., :MLA_NOPE], k_rope], axis=-1), p['k_qk']), rope)
    v = kv[..., MLA_NOPE:]

    heads = lambda t: jnp.pad(jnp.moveaxis(t, 2, 1), ((0, 0), (0, 0), (0, 0), (0, LANES - t.shape[-1])))
    qh, kh, vh = heads(q), heads(k), heads(v)
    scale = MLA_QK ** -0.5
    o_ctx = attention(qh, kh, vh, 0, lc, lc, scale)
    o_lat = attention(qh, kh, vh, lc, ll, l, scale)
    o = jnp.concatenate([o_ctx, o_lat], axis=2)[..., :MLA_V]
    o = jnp.moveaxis(o, 1, 2).reshape(b, l, MLA_HEADS * MLA_V)

    conv = conformer_conv(zconv, p['conv_w'], p['conv_b'], p['conv_ln_g'], p['conv_ln_b'],
                          ((0, lc), (lc, ll)))
    merged = jnp.concatenate([o, conv], axis=-1)
    return mm(merged.reshape(b * l, -1), p['w_out'].astype(BF16)).reshape(b, l, d)


def _odd_mixer(a, p, v_first, lc):
    b, l, d = a.shape
    nh = d // RW_HEAD
    pos = jnp.arange(l)[None, :, None]
    prev = jnp.where((pos == 0) | (pos == lc), 0.0, jnp.roll(a, 1, axis=1))
    nxt = jnp.where((pos == lc - 1) | (pos == l - 1), 0.0, jnp.roll(a, -1, axis=1))
    d_prev = prev - a
    d_next = nxt - a
    mix = lambda i: (a + d_prev * p['mu'][0, i] + d_next * p['mu'][1, i]).reshape(b * l, d)
    bf = lambda t: t.astype(BF16)
    cat2 = lambda t: jnp.concatenate([t[0], t[1]], axis=1)
    bdiag = lambda t: jnp.concatenate(
        [jnp.concatenate([t[0], jnp.zeros_like(t[0])], axis=1),
         jnp.concatenate([jnp.zeros_like(t[1]), t[1]], axis=1)], axis=0)

    xv = mix(3)
    k = mm(mix(2), bf(p['wk']))
    v = mm(xv, bf(p['wv']))
    if v_first is not None:
        gate = jax.nn.sigmoid(p['v0'] + mm(mm(xv, bf(p['v1'])), bf(p['v2'])))
        v = v + (v_first.reshape(b * l, d) - v) * gate
    kkf = (k * p['k_k']).reshape(b * l, nh, RW_HEAD)
    kk = (kkf / jnp.maximum(jnp.sqrt(jnp.sum(kkf * kkf, axis=-1, keepdims=True)), 1e-12)).reshape(b * l, d)
    w_lora = mm(mm(mix(1), bf(cat2(p['w1']))), bf(bdiag(p['w2'])), pre="tanh")
    a_lora = mm(mm(mix(4), bf(cat2(p['a1']))), bf(bdiag(p['a2'])))
    lws, kds, bds = [], [], []
    for di in range(2):
        w_log = -jax.nn.softplus(-(p['w0'][di] + w_lora[:, di * d:(di + 1) * d])) - 0.5
        lws.append(-jnp.exp(w_log))
        a_rate = jax.nn.sigmoid(p['a0'][di] + a_lora[:, di * d:(di + 1) * d])
        kds.append(k * (1 + (a_rate - 1) * p['k_a']))
        bds.append(kk * a_rate)
    r = mm(mix(0), bf(p['wr']))
    g = mm(mm(mix(5), bf(p['g1'])), bf(p['g2']), pre="sigmoid")

    st = lambda ts: jnp.stack(ts).reshape(2, b, l, d)
    s3 = lambda t: t.reshape(b, l, d)
    y2 = wkv_scan(st(lws), st(kds), st(bds), s3(r), s3(v), s3(kk), lc)
    y = (y2[0] + y2[1]).reshape(b * l, nh, RW_HEAD)

    mu = jnp.mean(y, axis=-1, keepdims=True)
    var = jnp.mean(jnp.square(y - mu), axis=-1, keepdims=True)
    yn = ((y - mu) * lax.rsqrt(var + RW_GN_EPS)).reshape(b * l, d)
    yn = yn * p['ln_g'] + p['ln_b']
    rk = p['r_k'].reshape(2, d)
    kmix = kds[0] * rk[0] + kds[1] * rk[1]
    coef = jnp.sum((r * kmix).reshape(b * l, nh, RW_HEAD), axis=-1, keepdims=True)
    bonus = (coef * v.reshape(b * l, nh, RW_HEAD)).reshape(b * l, d)
    out = mm((yn + bonus) * g, bf(p['wo'])).reshape(b, l, d)
    new_vf = v_first if v_first is not None else v.reshape(b, l, d)
    return out, new_vf


def kernel(x, c, ctx, c_ctx, ada_w, ada_b, norm_mix, norm_ffn, ffn_w1, ffn_w3, ffn_w2, even_w_in, mla_q_norm, mla_wq_b, mla_kv_norm, mla_wkv_b, mla_q_qk, mla_k_qk, conv_w, conv_b, conv_ln_g, conv_ln_b, even_w_out, rw_mu, rw_wr, rw_wk, rw_wv, rw_w0, rw_w1, rw_w2, rw_a0, rw_a1, rw_a2, rw_v0, rw_v1, rw_v2, rw_k_k, rw_k_a, rw_r_k, rw_g1, rw_g2, rw_ln_g, rw_ln_b, rw_wo):
    b, ll, d = x.shape
    lc = ctx.shape[1]
    l = lc + ll
    depth = ada_w.shape[0]
    h = jnp.concatenate([ctx, x], axis=1)
    rope = _rope_tables(lc, ll)
    is_ctx = (jnp.arange(l) < lc)[None, :, None]

    cond = jnp.concatenate([c, c_ctx[None, :]], axis=0)
    cond = jnp.pad(cond, ((0, -(-(b + 1) // 8) * 8 - (b + 1)), (0, 0)))
    v_first = None
    for i in range(depth):
        j = i // 2
        m = mm(cond, ada_w[i], pre="silu")[:b + 1] + ada_b[i]
        m_lat = m[:b].reshape(b, 1, 6, d)
        m_ctx = m[b].reshape(1, 1, 6, d)
        sel = lambda idx: jnp.where(is_ctx, m_ctx[:, :, idx], m_lat[:, :, idx])
        sh1, sc1, g1, sh2, sc2, g2 = (sel(t) for t in range(6))
        a = _rms(h, norm_mix[i]) * (1 + sc1) + sh1
        if i % 2 == 0:
            p = {'w_in': even_w_in[j], 'q_norm': mla_q_norm[j], 'wq_b': mla_wq_b[j],
                 'kv_norm': mla_kv_norm[j], 'wkv_b': mla_wkv_b[j], 'q_qk': mla_q_qk[j],
                 'k_qk': mla_k_qk[j], 'conv_w': conv_w[j], 'conv_b': conv_b[j],
                 'conv_ln_g': conv_ln_g[j], 'conv_ln_b': conv_ln_b[j], 'w_out': even_w_out[j]}
            o = _even_mixer(a, p, rope, lc)
        else:
            p = {'mu': rw_mu[j], 'wr': rw_wr[j], 'wk': rw_wk[j], 'wv': rw_wv[j],
                 'w0': rw_w0[j], 'w1': rw_w1[j], 'w2': rw_w2[j],
                 'a0': rw_a0[j], 'a1': rw_a1[j], 'a2': rw_a2[j],
                 'k_k': rw_k_k[j], 'k_a': rw_k_a[j], 'r_k': rw_r_k[j],
                 'g1': rw_g1[j], 'g2': rw_g2[j], 'ln_g': rw_ln_g[j], 'ln_b': rw_ln_b[j], 'wo': rw_wo[j]}
            if j > 0:
                p['v0'] = rw_v0[j - 1]
                p['v1'] = rw_v1[j - 1]
                p['v2'] = rw_v2[j - 1]
            o, v_first = _odd_mixer(a, p, v_first, lc)
        h = h + g1 * o
        f = (_rms(h, norm_ffn[i]) * (1 + sc2) + sh2).reshape(b * l, d)
        hid = mm_swiglu(f, ffn_w1[i].astype(BF16), ffn_w3[i].astype(BF16))
        h = h + g2 * mm(hid, ffn_w2[i].astype(BF16)).reshape(b, l, d)
    return h[:, lc:]
```

```python
import functools
import math

import jax
import jax.numpy as jnp
from jax import lax
from jax.experimental import pallas as pl
from jax.experimental.pallas import tpu as pltpu

F32 = jnp.float32
BF16 = jnp.bfloat16

EPS = 1e-6
GRID_W = 64
ROPE_THETA = 10000.0
MLA_HEADS = 8
MLA_NOPE = 64
MLA_ROPE = 32
MLA_QK = MLA_NOPE + MLA_ROPE
MLA_V = 64
AXIS_DIM = MLA_ROPE // 2
CONV_WIDTH = 31
RW_HEAD = 64
RW_GN_EPS = 6.4e-4

LANES = 128
VMEM_LIMIT = 48 * 1024 * 1024

SCAN_C = 64
SCAN_G = 4
SCAN_W = SCAN_G * RW_HEAD

NT_DIMS = (((1,), (1,)), ((), ()))
TN_DIMS = (((0,), (0,)), ((), ()))


def _pick(n, cands):
    for c in cands:
        if c <= n and n % c == 0:
            return c
    return n


def _mm_kernel(x_ref, w_ref, o_ref, *, pre):
    x = x_ref[...]
    if pre == "silu":
        x = x * jax.nn.sigmoid(x)
    elif pre == "tanh":
        x = jnp.tanh(x)
    elif pre == "sigmoid":
        x = jax.nn.sigmoid(x)
    o_ref[...] = jnp.dot(x.astype(BF16), w_ref[...].astype(BF16), preferred_element_type=F32)


def mm(x, w, pre=None):
    m, k = x.shape
    n = w.shape[1]
    tm = _pick(m, (512, 384, 256, 128, 64, 32, 16, 8))
    tn = n if n <= 1024 else _pick(n, (1024, 896, 768, 640, 512, 384, 256, 128))
    return pl.pallas_call(
        functools.partial(_mm_kernel, pre=pre),
        grid=(m // tm, n // tn),
        in_specs=[pl.BlockSpec((tm, k), lambda i, j: (i, 0)),
                  pl.BlockSpec((k, tn), lambda i, j: (0, j))],
        out_specs=pl.BlockSpec((tm, tn), lambda i, j: (i, j)),
        out_shape=jax.ShapeDtypeStruct((m, n), F32),
        compiler_params=pltpu.CompilerParams(
            dimension_semantics=("parallel", "parallel"), vmem_limit_bytes=VMEM_LIMIT),
        name="mm",
    )(x, w)


def _swiglu_kernel(x_ref, w1_ref, w3_ref, o_ref):
    x = x_ref[...].astype(BF16)
    a = jnp.dot(x, w1_ref[...], preferred_element_type=F32)
    b = jnp.dot(x, w3_ref[...], preferred_element_type=F32)
    o_ref[...] = a * jax.nn.sigmoid(a) * b


def mm_swiglu(x, w1, w3):
    m, k = x.shape
    n = w1.shape[1]
    tm = _pick(m, (512, 384, 256, 128, 64, 32, 16, 8))
    tn = _pick(n, (1408, 1024, 768, 512, 384, 256, 128))
    return pl.pallas_call(
        _swiglu_kernel,
        grid=(m // tm, n // tn),
        in_specs=[pl.BlockSpec((tm, k), lambda i, j: (i, 0)),
                  pl.BlockSpec((k, tn), lambda i, j: (0, j)),
                  pl.BlockSpec((k, tn), lambda i, j: (0, j))],
        out_specs=pl.BlockSpec((tm, tn), lambda i, j: (i, j)),
        out_shape=jax.ShapeDtypeStruct((m, n), F32),
        compiler_params=pltpu.CompilerParams(
            dimension_semantics=("parallel", "parallel"), vmem_limit_bytes=VMEM_LIMIT),
        name="mm_swiglu",
    )(x, w1, w3)


def _attn_kernel(q_ref, k_ref, v_ref, o_ref, *, scale):
    q = q_ref[0, 0].astype(BF16)
    k = k_ref[0, 0].astype(BF16)
    v = v_ref[0, 0].astype(BF16)
    s = lax.dot_general(q, k, NT_DIMS, preferred_element_type=F32) * scale
    m = jnp.max(s, axis=-1, keepdims=True)
    p = jnp.exp(s - m)
    l = jnp.sum(p, axis=-1, keepdims=True)
    o = jnp.dot(p.astype(BF16), v, preferred_element_type=F32)
    o_ref[0, 0] = o / l


def attention(q, k, v, q_start, q_len, k_len, scale):
    b, h, _, dp = q.shape
    tq = _pick(math.gcd(q_len, q_start), (256, 128, 64, 32, 16, 8))
    q0 = q_start // tq
    return pl.pallas_call(
        functools.partial(_attn_kernel, scale=scale),
        grid=(b, h, q_len // tq),
        in_specs=[pl.BlockSpec((1, 1, tq, dp), lambda bi, hi, i: (bi, hi, i + q0, 0)),
                  pl.BlockSpec((1, 1, k_len, dp), lambda bi, hi, i: (bi, hi, 0, 0)),
                  pl.BlockSpec((1, 1, k_len, dp), lambda bi, hi, i: (bi, hi, 0, 0))],
        out_specs=pl.BlockSpec((1, 1, tq, dp), lambda bi, hi, i: (bi, hi, i, 0)),
        out_shape=jax.ShapeDtypeStruct((b, h, q_len, dp), F32),
        compiler_params=pltpu.CompilerParams(
            dimension_semantics=("parallel", "parallel", "parallel"), vmem_limit_bytes=VMEM_LIMIT),
        name="attention",
    )(q, k, v)


CONV_ROWS = 64
CONV_HALO = 16


def _conv_kernel(u_ref, w_ref, b_ref, g_ref, beta_ref, o_ref, pad_ref, y_ref, *, segments, ch):
    half = CONV_WIDTH // 2
    win = CONV_ROWS + 2 * CONV_HALO
    nlb = ch // LANES
    for seg_start, seg_len in segments:
        zeros = jnp.zeros((CONV_HALO, ch), F32)
        pad_ref[0:CONV_HALO, :] = zeros
        pad_ref[CONV_HALO + seg_len:2 * CONV_HALO + seg_len, :] = zeros

        def glu_body(i, carry):
            r0 = pl.multiple_of(i * CONV_ROWS, CONV_ROWS)
            u = u_ref[0, pl.ds(seg_start + r0, CONV_ROWS), :]
            pad_ref[pl.ds(CONV_HALO + r0, CONV_ROWS), :] = u[:, :ch] * jax.nn.sigmoid(u[:, ch:])
            return carry

        lax.fori_loop(0, seg_len // CONV_ROWS, glu_body, 0)

        def conv_body(i, carry):
            r0 = pl.multiple_of(i * CONV_ROWS, CONV_ROWS)
            for lb in range(nlb):
                ls = slice(lb * LANES, (lb + 1) * LANES)
                xwin = pad_ref[pl.ds(r0, win), ls]
                acc = jnp.zeros((CONV_ROWS, LANES), F32)
                for j in range(CONV_WIDTH):
                    off = CONV_HALO - half + j
                    shifted = pltpu.roll(xwin, shift=(win - off) % win, axis=0)[0:CONV_ROWS]
                    acc = acc + shifted * w_ref[j:j + 1, ls]
                y_ref[pl.ds(seg_start + r0, CONV_ROWS), ls] = acc + b_ref[0:1, ls]
            return carry

        lax.fori_loop(0, seg_len // CONV_ROWS, conv_body, 0)

    def ln_body(i, carry):
        r0 = pl.multiple_of(i * CONV_ROWS, CONV_ROWS)
        y = y_ref[pl.ds(r0, CONV_ROWS), :]
        mu = jnp.mean(y, axis=-1, keepdims=True)
        d = y - mu
        var = jnp.mean(d * d, axis=-1, keepdims=True)
        z = d * lax.rsqrt(var + 1e-5) * g_ref[0:1, :] + beta_ref[0:1, :]
        o_ref[0, pl.ds(r0, CONV_ROWS), :] = z * jax.nn.sigmoid(z)
        return carry

    total = sum(s[1] for s in segments)
    lax.fori_loop(0, total // CONV_ROWS, ln_body, 0)


def conformer_conv(u, conv_w, conv_b, ln_g, ln_b, segments):
    b, l, ch2 = u.shape
    ch = ch2 // 2
    max_seg = max(s[1] for s in segments)
    row = lambda t: t.reshape(1, ch)
    return pl.pallas_call(
        functools.partial(_conv_kernel, segments=segments, ch=ch),
        grid=(b,),
        in_specs=[pl.BlockSpec((1, l, ch2), lambda i: (i, 0, 0)),
                  pl.BlockSpec((CONV_WIDTH, ch), lambda i: (0, 0)),
                  pl.BlockSpec((1, ch), lambda i: (0, 0)),
                  pl.BlockSpec((1, ch), lambda i: (0, 0)),
                  pl.BlockSpec((1, ch), lambda i: (0, 0))],
        out_specs=pl.BlockSpec((1, l, ch), lambda i: (i, 0, 0)),
        out_shape=jax.ShapeDtypeStruct((b, l, ch), F32),
        scratch_shapes=[pltpu.VMEM((max_seg + 2 * CONV_HALO, ch), F32),
                        pltpu.VMEM((l, ch), F32)],
        compiler_params=pltpu.CompilerParams(
            dimension_semantics=("parallel",), vmem_limit_bytes=VMEM_LIMIT),
        name="conformer_conv",
    )(u, conv_w, row(conv_b), row(ln_g), row(ln_b))


def _bdot(a, b, dims=None):
    a = a.astype(BF16)
    b = b.astype(BF16)
    if dims is None:
        return jnp.dot(a, b, preferred_element_type=F32)
    return lax.dot_general(a, b, dims, preferred_element_type=F32)


def _tile_rows(x, mask):
    return jnp.concatenate([x.astype(BF16)] * SCAN_G, axis=0) * mask


def _scan_kernel(lw_ref, kd_ref, bd_ref, r_ref, v_ref, kk_ref, cum_ref, strict_ref, incl_ref, lvl_ref,
                 blk_ref, y_ref, state_ref):
    c = SCAN_C
    @pl.when(pl.program_id(2) == 0)
    def _():
        state_ref[...] = jnp.zeros_like(state_ref)

    cum2 = cum_ref[0]
    strict = strict_ref[0]
    incl = incl_ref[0]
    blk = blk_ref[...]
    groups = range(lw_ref.shape[-1] // SCAN_W)
    ls = lambda g: slice(g * SCAN_W, (g + 1) * SCAN_W)
    bf = lambda t: t.astype(BF16)

    lw = [lw_ref[0, 0, :, ls(g)] for g in groups]
    lp = []
    for g in groups:
        hi = bf(lw[g])
        lo = bf(lw[g] - hi.astype(F32))
        lp.append(_bdot(cum2, jnp.concatenate([hi, lo], axis=0)))

    total, ar, bk, bk_tail = [], [], [], []
    for g in groups:
        tot = jnp.sum(lw[g], axis=0, keepdims=True)
        inv = jnp.exp(-lp[g])
        tail = jnp.exp(tot - lp[g])
        kd = kd_ref[0, 0, :, ls(g)]
        bd = bd_ref[0, 0, :, ls(g)]
        at = -kk_ref[0, :, ls(g)] * jnp.exp(lp[g] - lw[g])
        rt = r_ref[0, :, ls(g)] * jnp.exp(lp[g])
        total.append(tot)
        ar.append(bf(jnp.concatenate([at, rt], axis=0)))
        bk.append(jnp.concatenate([_tile_rows(bd * inv, blk), _tile_rows(kd * inv, blk)], axis=0))
        bk_tail.append(bf(jnp.concatenate([bd * tail, kd * tail], axis=0)))

    gg = [_bdot(ar[g], bk[g], NT_DIMS) for g in groups]
    sh = [_bdot(ar[g], bf(state_ref[g]), NT_DIMS) for g in groups]

    gc = SCAN_G * c
    vext = [_tile_rows(v_ref[0, :, ls(g)], blk) for g in groups]
    rhs = [sh[g][:c] + _bdot(bf(gg[g][:c, gc:] * strict), vext[g]) for g in groups]

    m = [gg[g][:c, :gc] * strict for g in groups]
    for lv in range(lvl_ref.shape[1]):
        m = [m[g] + _bdot(bf(m[g] * lvl_ref[0, lv]), _tile_rows(m[g], blk)) for g in groups]
    u = [rhs[g] + _bdot(bf(m[g]), _tile_rows(rhs[g], blk)) for g in groups]

    for g in groups:
        rbk = bf(gg[g][c:] * jnp.concatenate([incl, incl], axis=1))
        y = sh[g][c:] + _bdot(rbk, jnp.concatenate([_tile_rows(u[g], blk), vext[g]], axis=0))
        y_ref[0, 0, :, ls(g)] = y
    for g in groups:
        uv = bf(jnp.concatenate([u[g], v_ref[0, :, ls(g)]], axis=0))
        upd = _bdot(uv, bk_tail[g], TN_DIMS)
        state_ref[g] = (state_ref[g] * jnp.exp(total[g]) + upd) * blk.astype(F32)


def _scan_masks():
    c, g = SCAN_C, SCAN_G
    t = jnp.arange(c)[:, None]
    s = jnp.arange(c)[None, :]
    cum = jnp.stack([s <= t, s >= t])
    cum = jnp.concatenate([cum, cum], axis=2).astype(BF16)
    sg = jnp.arange(g * c)[None, :] % c
    strict = jnp.stack([sg < t, sg > t]).astype(F32)
    incl = jnp.stack([sg <= t, sg >= t]).astype(F32)
    sizes = [1 << i for i in range(c.bit_length() - 1)]
    lvl = jnp.stack([((t // (2 * s) == sg // (2 * s)) & (t // s != sg // s)).astype(F32) for s in sizes])
    lvl = strict[:, None] * lvl[None]
    rb = jnp.arange(SCAN_W)[:, None] // RW_HEAD
    cb = jnp.arange(SCAN_W)[None, :] // RW_HEAD
    blk = (rb == cb).astype(BF16)
    return cum, strict, incl, lvl, blk


def wkv_scan(lw, kd, bd, r, v, kk, lc):
    _, b, l, d = lw.shape
    c = SCAN_C
    nc = l // c
    ncc = lc // c
    cum, strict, incl, lvl, blk = _scan_masks()
    nlv = lvl.shape[1]

    def chunk(di, ci):
        rev = jnp.where(ci < ncc, ncc - 1 - ci, nc - 1 + ncc - ci)
        return jnp.where(di == 0, ci, rev)

    dspec = pl.BlockSpec((1, 1, c, d), lambda bi, di, ci: (di, bi, chunk(di, ci), 0))
    sspec = pl.BlockSpec((1, c, d), lambda bi, di, ci: (bi, chunk(di, ci), 0))
    return pl.pallas_call(
        _scan_kernel,
        grid=(b, 2, nc),
        in_specs=[dspec, dspec, dspec, sspec, sspec, sspec,
                  pl.BlockSpec((1, c, 2 * c), lambda bi, di, ci: (di, 0, 0)),
                  pl.BlockSpec((1, c, SCAN_G * c), lambda bi, di, ci: (di, 0, 0)),
                  pl.BlockSpec((1, c, SCAN_G * c), lambda bi, di, ci: (di, 0, 0)),
                  pl.BlockSpec((1, nlv, c, SCAN_G * c), lambda bi, di, ci: (di, 0, 0, 0)),
                  pl.BlockSpec((SCAN_W, SCAN_W), lambda bi, di, ci: (0, 0))],
        out_specs=dspec,
        out_shape=jax.ShapeDtypeStruct((2, b, l, d), F32),
        scratch_shapes=[pltpu.VMEM((d // SCAN_W, SCAN_W, SCAN_W), F32)],
        compiler_params=pltpu.CompilerParams(
            dimension_semantics=("parallel", "parallel", "arbitrary"), vmem_limit_bytes=VMEM_LIMIT),
        name="wkv_scan",
    )(lw, kd, bd, r, v, kk, cum, strict, incl, lvl, blk)


def _rms(x, g):
    return x * lax.rsqrt(jnp.mean(x * x, axis=-1, keepdims=True) + EPS) * g


def _rope_tables(lc, ll):
    rows = ll // GRID_W
    row = jnp.repeat(jnp.arange(rows, dtype=F32), GRID_W)
    col = jnp.tile(jnp.arange(GRID_W, dtype=F32), rows)
    inv = ROPE_THETA ** (-jnp.arange(0, AXIS_DIM, 2, dtype=F32) / AXIS_DIM)
    ang_r = row[:, None] * inv
    ang_c = col[:, None] * inv
    pad = lambda t, fill: jnp.concatenate([jnp.full((lc, t.shape[1]), fill, F32), t], axis=0)
    return (pad(jnp.cos(ang_r), 1.0), pad(jnp.sin(ang_r), 0.0),
            pad(jnp.cos(ang_c), 1.0), pad(jnp.sin(ang_c), 0.0))


def _rotate(x, cos, sin):
    m = x.shape[-1] // 2
    x1, x2 = x[..., :m], x[..., m:]
    cos = cos[:, None, :]
    sin = sin[:, None, :]
    return jnp.concatenate([x1 * cos - x2 * sin, x2 * cos + x1 * sin], axis=-1)


def _rope(t, tabs):
    cr, sr, cc, sc = tabs
    return jnp.concatenate([t[..., :MLA_NOPE],
                            _rotate(t[..., MLA_NOPE:MLA_NOPE + AXIS_DIM], cr, sr),
                            _rotate(t[..., MLA_NOPE + AXIS_DIM:], cc, sc)], axis=-1)


def _even_mixer(a, p, rope, lc):
    b, l, d = a.shape
    ll = l - lc
    q_rank = p['q_norm'].shape[0]
    kv_rank = p['kv_norm'].shape[0]
    n_in = p['w_in'].shape[1]
    n_pad = -(-n_in // LANES) * LANES
    w_in = jnp.pad(p['w_in'], ((0, 0), (0, n_pad - n_in))).astype(BF16)
    z = mm(a.reshape(b * l, d), w_in).reshape(b, l, n_pad)
    zq = z[..., :q_rank]
    zkv = z[..., q_rank:q_rank + kv_rank]
    zr = z[..., q_rank + kv_rank:q_rank + kv_rank + MLA_ROPE]
    zconv = z[..., q_rank + kv_rank + MLA_ROPE:n_in]

    q = mm(_rms(zq, p['q_norm']).reshape(b * l, q_rank), p['wq_b'].astype(BF16))
    q = _rope(_rms(q.reshape(b, l, MLA_HEADS, MLA_QK), p['q_qk']), rope)
    kv = mm(_rms(zkv, p['kv_norm']).reshape(b * l, kv_rank), p['wkv_b'].astype(BF16))
    kv = kv.reshape(b, l, MLA_HEADS, MLA_NOPE + MLA_V)
    k_rope = jnp.broadcast_to(zr[:, :, None, :], (b, l, MLA_HEADS, MLA_ROPE))
    k = _rope(_rms(jnp.concatenate([kv[..., :MLA_NOPE], k_rope], axis=-1), p['k_qk']), rope)
    v = kv[..., MLA_NOPE:]

    heads = lambda t: jnp.pad(jnp.moveaxis(t, 2, 1), ((0, 0), (0, 0), (0, 0), (0, LANES - t.shape[-1])))
    qh, kh, vh = heads(q), heads(k), heads(v)
    scale = MLA_QK ** -0.5
    o_ctx = attention(qh, kh, vh, 0, lc, lc, scale)
    o_lat = attention(qh, kh, vh, lc, ll, l, scale)
    o = jnp.concatenate([o_ctx, o_lat], axis=2)[..., :MLA_V]
    o = jnp.moveaxis(o, 1, 2).reshape(b, l, MLA_HEADS * MLA_V)

    conv = conformer_conv(zconv, p['conv_w'], p['conv_b'], p['conv_ln_g'], p['conv_ln_b'],
                          ((0, lc), (lc, ll)))
    merged = jnp.concatenate([o, conv], axis=-1)
    return mm(merged.reshape(b * l, -1), p['w_out'].astype(BF16)).reshape(b, l, d)


def _odd_mixer(a, p, v_first, lc):
    b, l, d = a.shape
    nh = d // RW_HEAD
    pos = jnp.arange(l)[None, :, None]
    prev = jnp.where((pos == 0) | (pos == lc), 0.0, jnp.roll(a, 1, axis=1))
    nxt = jnp.where((pos == lc - 1) | (pos == l - 1), 0.0, jnp.roll(a, -1, axis=1))
    d_prev = prev - a
    d_next = nxt - a
    mix = lambda i: (a + d_prev * p['mu'][0, i] + d_next * p['mu'][1, i]).reshape(b * l, d)
    bf = lambda t: t.astype(BF16)
    cat2 = lambda t: jnp.concatenate([t[0], t[1]], axis=1)
    bdiag = lambda t: jnp.concatenate(
        [jnp.concatenate([t[0], jnp.zeros_like(t[0])], axis=1),
         jnp.concatenate([jnp.zeros_like(t[1]), t[1]], axis=1)], axis=0)

    xv = mix(3)
    k = mm(mix(2), bf(p['wk']))
    v = mm(xv, bf(p['wv']))
    if v_first is not None:
        gate = jax.nn.sigmoid(p['v0'] + mm(mm(xv, bf(p['v1'])), bf(p['v2'])))
        v = v + (v_first.reshape(b * l, d) - v) * gate
    kkf = (k * p['k_k']).reshape(b * l, nh, RW_HEAD)
    kk = (kkf / jnp.maximum(jnp.sqrt(jnp.sum(kkf * kkf, axis=-1, keepdims=True)), 1e-12)).reshape(b * l, d)
    w_lora = mm(mm(mix(1), bf(cat2(p['w1']))), bf(bdiag(p['w2'])), pre="tanh")
    a_lora = mm(mm(mix(4), bf(cat2(p['a1']))), bf(bdiag(p['a2'])))
    lws, kds, bds = [], [], []
    for di in range(2):
        w_log = -jax.nn.softplus(-(p['w0'][di] + w_lora[:, di * d:(di + 1) * d])) - 0.5
        lws.append(-jnp.exp(w_log))
        a_rate = jax.nn.sigmoid(p['a0'][di] + a_lora[:, di * d:(di + 1) * d])
        kds.append(k * (1 + (a_rate - 1) * p['k_a']))
        bds.append(kk * a_rate)
    r = mm(mix(0), bf(p['wr']))
    g = mm(mm(mix(5), bf(p['g1'])), bf(p['g2']), pre="sigmoid")

    st = lambda ts: jnp.stack(ts).reshape(2, b, l, d)
    s3 = lambda t: t.reshape(b, l, d)
    y2 = wkv_scan(st(lws), st(kds), st(bds), s3(r), s3(v), s3(kk), lc)
    y = (y2[0] + y2[1]).reshape(b * l, nh, RW_HEAD)

    mu = jnp.mean(y, axis=-1, keepdims=True)
    var = jnp.mean(jnp.square(y - mu), axis=-1, keepdims=True)
    yn = ((y - mu) * lax.rsqrt(var + RW_GN_EPS)).reshape(b * l, d)
    yn = yn * p['ln_g'] + p['ln_b']
    rk = p['r_k'].reshape(2, d)
    kmix = kds[0] * rk[0] + kds[1] * rk[1]
    coef = jnp.sum((r * kmix).reshape(b * l, nh, RW_HEAD), axis=-1, keepdims=True)
    bonus = (coef * v.reshape(b * l, nh, RW_HEAD)).reshape(b * l, d)
    out = mm((yn + bonus) * g, bf(p['wo'])).reshape(b, l, d)
    new_vf = v_first if v_first is not None else v.reshape(b, l, d)
    return out, new_vf


def kernel(x, c, ctx, c_ctx, ada_w, ada_b, norm_mix, norm_ffn, ffn_w1, ffn_w3, ffn_w2, even_w_in, mla_q_norm, mla_wq_b, mla_kv_norm, mla_wkv_b, mla_q_qk, mla_k_qk, conv_w, conv_b, conv_ln_g, conv_ln_b, even_w_out, rw_mu, rw_wr, rw_wk, rw_wv, rw_w0, rw_w1, rw_w2, rw_a0, rw_a1, rw_a2, rw_v0, rw_v1, rw_v2, rw_k_k, rw_k_a, rw_r_k, rw_g1, rw_g2, rw_ln_g, rw_ln_b, rw_wo):
    b, ll, d = x.shape
    lc = ctx.shape[1]
    l = lc + ll
    depth = ada_w.shape[0]
    h = jnp.concatenate([ctx, x], axis=1)
    rope = _rope_tables(lc, ll)
    is_ctx = (jnp.arange(l) < lc)[None, :, None]

    cond = jnp.concatenate([c, c_ctx[None, :]], axis=0)
    cond = jnp.pad(cond, ((0, -(-(b + 1) // 8) * 8 - (b + 1)), (0, 0)))
    v_first = None
    for i in range(depth):
        j = i // 2
        m = mm(cond, ada_w[i], pre="silu")[:b + 1] + ada_b[i]
        m_lat = m[:b].reshape(b, 1, 6, d)
        m_ctx = m[b].reshape(1, 1, 6, d)
        sel = lambda idx: jnp.where(is_ctx, m_ctx[:, :, idx], m_lat[:, :, idx])
        sh1, sc1, g1, sh2, sc2, g2 = (sel(t) for t in range(6))
        a = _rms(h, norm_mix[i]) * (1 + sc1) + sh1
        if i % 2 == 0:
            p = {'w_in': even_w_in[j], 'q_norm': mla_q_norm[j], 'wq_b': mla_wq_b[j],
                 'kv_norm': mla_kv_norm[j], 'wkv_b': mla_wkv_b[j], 'q_qk': mla_q_qk[j],
                 'k_qk': mla_k_qk[j], 'conv_w': conv_w[j], 'conv_b': conv_b[j],
                 'conv_ln_g': conv_ln_g[j], 'conv_ln_b': conv_ln_b[j], 'w_out': even_w_out[j]}
            o = _even_mixer(a, p, rope, lc)
        else:
            p = {'mu': rw_mu[j], 'wr': rw_wr[j], 'wk': rw_wk[j], 'wv': rw_wv[j],
                 'w0': rw_w0[j], 'w1': rw_w1[j], 'w2': rw_w2[j],
                 'a0': rw_a0[j], 'a1': rw_a1[j], 'a2': rw_a2[j],
                 'k_k': rw_k_k[j], 'k_a': rw_k_a[j], 'r_k': rw_r_k[j],
                 'g1': rw_g1[j], 'g2': rw_g2[j], 'ln_g': rw_ln_g[j], 'ln_b': rw_ln_b[j], 'wo': rw_wo[j]}
            if j > 0:
                p['v0'] = rw_v0[j - 1]
                p['v1'] = rw_v1[j - 1]
                p['v2'] = rw_v2[j - 1]
            o, v_first = _odd_mixer(a, p, v_first, lc)
        h = h + g1 * o
        f = (_rms(h, norm_ffn[i]) * (1 + sc2) + sh2).reshape(b * l, d)
        hid = mm_swiglu(f, ffn_w1[i].astype(BF16), ffn_w3[i].astype(BF16))
        h = h + g2 * mm(hid, ffn_w2[i].astype(BF16)).reshape(b, l, d)
    return h[:, lc:]
```

```python
import functools

import jax
import jax.numpy as jnp
from jax import lax
from jax.experimental import pallas as pl
from jax.experimental.pallas import tpu as pltpu

F32 = jnp.float32
BF16 = jnp.bfloat16

EPS = 1e-6
GRID_W = 64
ROPE_THETA = 10000.0
MLA_HEADS = 8
MLA_NOPE = 64
MLA_ROPE = 32
MLA_QK = MLA_NOPE + MLA_ROPE
MLA_V = 64
AXIS_DIM = MLA_ROPE // 2
CONV_WIDTH = 31
CONV_LN_EPS = 1e-5
RW_HEAD = 64
RW_GN_EPS = 6.4e-4

LANES = 128
SUBLANES = 8
VMEM_LIMIT = 56 * 1024 * 1024

SCAN_C = 64
SCAN_G = 4
SCAN_W = SCAN_G * RW_HEAD

NT_DIMS = (((1,), (1,)), ((), ()))
TN_DIMS = (((0,), (0,)), ((), ()))


def _pick(n, cands):
    for c in cands:
        if c <= n and n % c == 0:
            return c
    return n


def _const_spec(shape):
    nd = len(shape)
    return pl.BlockSpec(shape, lambda *_: (0,) * nd, pipeline_mode=pl.Buffered(1))


def _params(*sem):
    return pltpu.CompilerParams(dimension_semantics=sem, vmem_limit_bytes=VMEM_LIMIT)


def _bdot(a, b, dims=None):
    a = a.astype(BF16)
    b = b.astype(BF16)
    if dims is None:
        return jnp.dot(a, b, preferred_element_type=F32)
    return lax.dot_general(a, b, dims, preferred_element_type=F32)


def _dot2(x, w):
    hi = x.astype(BF16)
    lo = (x - hi.astype(F32)).astype(BF16)
    return _bdot(hi, w) + _bdot(lo, w)


def _row_select(mod_ref, first_row, n, lc):
    rows = first_row + lax.broadcasted_iota(jnp.int32, (n, 1), 0)
    is_ctx = rows < lc
    return lambda i: jnp.where(is_ctx, mod_ref[0, 0, i:i + 1, :], mod_ref[0, 1, i:i + 1, :])


def _norm_mod(h, gain, scale, shift):
    return h * lax.rsqrt(jnp.mean(h * h, axis=-1, keepdims=True) + EPS) * gain * (1 + scale) + shift


def _sigmoid(x):
    return 1.0 / (1.0 + jnp.exp(-x))


def _mm_kernel(x_ref, w_ref, o_ref, *, pre):
    x = x_ref[...]
    if pre == "silu":
        x = x * _sigmoid(x)
    o_ref[...] = jnp.dot(x.astype(BF16), w_ref[...].astype(BF16), preferred_element_type=F32)


def mm(x, w, pre=None):
    m, k = x.shape
    n = w.shape[1]
    tm = _pick(m, (512, 384, 256, 128, 64, 32, 16, 8))
    tn = n if n <= 1024 else _pick(n, (1024, 896, 768, 640, 512, 384, 256, 128))
    return pl.pallas_call(
        functools.partial(_mm_kernel, pre=pre),
        grid=(m // tm, n // tn),
        in_specs=[pl.BlockSpec((tm, k), lambda i, j: (i, 0)),
                  pl.BlockSpec((k, tn), lambda i, j: (0, j))],
        out_specs=pl.BlockSpec((tm, tn), lambda i, j: (i, j)),
        out_shape=jax.ShapeDtypeStruct((m, n), F32),
        compiler_params=_params("parallel", "parallel"),
        name="mm",
    )(x, w)


def _proj_ffn_kernel(*refs, nx, lc, tm, hid_chunk):
    h_ref, mod_ref, gn_ref = refs[0], refs[1], refs[2]
    x_refs = refs[3:3 + nx]
    wp_refs = refs[3 + nx:3 + 2 * nx]
    w1_ref, w3_ref, w2_ref, o_ref = refs[3 + 2 * nx:]
    sel = _row_select(mod_ref, pl.program_id(1) * tm, tm, lc)
    proj = _bdot(x_refs[0][0], wp_refs[0][...])
    for i in range(1, nx):
        proj = proj + _bdot(x_refs[i][0], wp_refs[i][...])
    h1 = h_ref[0] + sel(2) * proj
    f = _norm_mod(h1, gn_ref[...], sel(4), sel(3)).astype(BF16)
    acc = None
    for c0 in range(0, w1_ref.shape[1], hid_chunk):
        a = _bdot(f, w1_ref[:, c0:c0 + hid_chunk])
        b = _bdot(f, w3_ref[:, c0:c0 + hid_chunk])
        part = _bdot(a * _sigmoid(a) * b, w2_ref[c0:c0 + hid_chunk, :])
        acc = part if acc is None else acc + part
    o_ref[0] = h1 + sel(5) * acc


def proj_ffn(h, xs, wps, mod, gn, w1, w3, w2, lc):
    b, l, d = h.shape
    tm = _pick(l, (384, 256, 128, 64, 32, 16, 8))
    nh = w1.shape[1]
    hid_chunk = nh // 2 if (nh // 2) % LANES == 0 else nh
    row = lambda t: pl.BlockSpec((1, tm, t.shape[-1]), lambda bi, j: (bi, j, 0))
    return pl.pallas_call(
        functools.partial(_proj_ffn_kernel, nx=len(xs), lc=lc, tm=tm, hid_chunk=hid_chunk),
        grid=(b, l // tm),
        in_specs=[row(h), pl.BlockSpec((1, 2, 6, d), lambda bi, j: (bi, 0, 0, 0)), _const_spec((1, d))]
                 + [row(x) for x in xs] + [_const_spec(w.shape) for w in wps]
                 + [_const_spec(w1.shape), _const_spec(w3.shape), _const_spec(w2.shape)],
        out_specs=row(h),
        out_shape=jax.ShapeDtypeStruct(h.shape, F32),
        compiler_params=_params("parallel", "parallel"),
        name="proj_ffn",
    )(h, mod, gn.reshape(1, d), *xs, *wps, w1, w3, w2)


def _mla_prep_kernel(h_ref, mod_ref, gn_ref, win_ref, qn_ref, kvn_ref, wq_ref, wk_ref, wv_ref,
                     qg_ref, kg_ref, ones_ref, cos_ref, sin_ref,
                     q_ref, k_ref, v_ref, zc_ref, *, lc, tm, q_rank, kv_rank, scale):
    sel = _row_select(mod_ref, pl.program_id(1) * tm, tm, lc)
    a = _norm_mod(h_ref[0], gn_ref[...], sel(1), sel(0))
    z = _bdot(a, win_ref[...])
    rms = lambda t, g: t * lax.rsqrt(jnp.mean(t * t, axis=-1, keepdims=True) + EPS) * g
    zq = rms(z[:, :q_rank], qn_ref[...]).astype(BF16)
    zkv = rms(z[:, q_rank:q_rank + kv_rank], kvn_ref[...]).astype(BF16)
    slots = wq_ref.shape[1]
    k0 = q_rank + kv_rank
    q = _bdot(zq, wq_ref[...])
    k = _bdot(zkv, wk_ref[...]) + z[:, k0:k0 + slots]
    v_ref[0] = _bdot(zkv, wv_ref[...]).astype(BF16)
    zc_ref[0] = z[:, k0 + slots:]

    lane = lax.broadcasted_iota(jnp.int32, (1, slots), 1) % LANES
    first_half = ((lane - MLA_NOPE) % AXIS_DIM) < AXIS_DIM // 2
    cos = cos_ref[...]
    sin = sin_ref[...]

    def head_norm_rope(t, gain):
        ss = _bdot(t * t, ones_ref[...]) * (1.0 / MLA_QK)
        t = t * lax.rsqrt(ss + EPS) * gain
        partner = jnp.where(first_half, pltpu.roll(t, slots - AXIS_DIM // 2, axis=1),
                            pltpu.roll(t, AXIS_DIM // 2, axis=1))
        return t * cos + partner * sin

    q_ref[0] = (head_norm_rope(q, qg_ref[...]) * scale).astype(BF16)
    k_ref[0] = head_norm_rope(k, kg_ref[...]).astype(BF16)


def _head_slots(w, width):
    kdim = w.shape[0]
    w = w.reshape(kdim, MLA_HEADS, width)
    return jnp.pad(w, ((0, 0), (0, 0), (0, LANES - width))).reshape(kdim, MLA_HEADS * LANES)


def _rope_slot_tables(lc, ll):
    rows = ll // GRID_W
    row = jnp.repeat(jnp.arange(rows, dtype=F32), GRID_W)
    col = jnp.tile(jnp.arange(GRID_W, dtype=F32), rows)
    inv = ROPE_THETA ** (-jnp.arange(0, AXIS_DIM, 2, dtype=F32) / AXIS_DIM)
    half = AXIS_DIM // 2
    cos_parts, sin_parts = [], []
    for pos in (row, col):
        ang = pos[:, None] * inv
        cos_parts += [jnp.cos(ang), jnp.cos(ang)]
        sin_parts += [-jnp.sin(ang), jnp.sin(ang)]
    ones = jnp.ones((ll, MLA_NOPE), F32)
    zeros = jnp.zeros((ll, MLA_NOPE), F32)
    tail1 = jnp.ones((ll, LANES - MLA_QK), F32)
    tail0 = jnp.zeros((ll, LANES - MLA_QK), F32)
    cos = jnp.concatenate([ones] + cos_parts + [tail1], axis=1)
    sin = jnp.concatenate([zeros] + sin_parts + [tail0], axis=1)
    cos = jnp.concatenate([jnp.ones((lc, LANES), F32), cos], axis=0)
    sin = jnp.concatenate([jnp.zeros((lc, LANES), F32), sin], axis=0)
    del half
    return jnp.tile(cos, (1, MLA_HEADS)), jnp.tile(sin, (1, MLA_HEADS))


def mla_prep(h, mod, gn, p, tables, lc):
    b, l, d = h.shape
    q_rank = p['q_norm'].shape[0]
    kv_rank = p['kv_norm'].shape[0]
    slots = MLA_HEADS * LANES
    w_in = p['w_in']
    k0 = q_rank + kv_rank
    w_rope = jnp.tile(jnp.pad(w_in[:, k0:k0 + MLA_ROPE], ((0, 0), (MLA_NOPE, LANES - MLA_QK))), (1, MLA_HEADS))
    w_ext = jnp.concatenate([w_in[:, :k0], w_rope, w_in[:, k0 + MLA_ROPE:]], axis=1).astype(BF16)
    n_conv = w_in.shape[1] - k0 - MLA_ROPE
    wkv = p['wkv_b'].reshape(kv_rank, MLA_HEADS, MLA_NOPE + MLA_V)
    wq = _head_slots(p['wq_b'], MLA_QK).astype(BF16)
    wk = _head_slots(wkv[:, :, :MLA_NOPE].reshape(kv_rank, -1), MLA_NOPE).astype(BF16)
    wv = wkv[:, :, MLA_NOPE:].reshape(kv_rank, MLA_HEADS * MLA_V).astype(BF16)
    slot_gain = lambda g: jnp.tile(jnp.pad(g, (0, LANES - MLA_QK)), MLA_HEADS).reshape(1, slots)
    sid = jnp.arange(slots) // LANES
    ones = (sid[:, None] == sid[None, :]).astype(BF16)
    tm = _pick(l, (384, 256, 128, 64, 32, 16, 8))
    cos, sin = tables
    row = lambda n: pl.BlockSpec((1, tm, n), lambda bi, j: (bi, j, 0))
    tab = pl.BlockSpec((tm, slots), lambda bi, j: (j, 0))
    consts = [gn.reshape(1, d), w_ext, p['q_norm'].reshape(1, -1), p['kv_norm'].reshape(1, -1), wq, wk, wv,
              slot_gain(p['q_qk']), slot_gain(p['k_qk']), ones]
    return pl.pallas_call(
        functools.partial(_mla_prep_kernel, lc=lc, tm=tm, q_rank=q_rank, kv_rank=kv_rank, scale=MLA_QK ** -0.5),
        grid=(b, l // tm),
        in_specs=[row(d), pl.BlockSpec((1, 2, 6, d), lambda bi, j: (bi, 0, 0, 0))]
                 + [_const_spec(t.shape) for t in consts] + [tab, tab],
        out_specs=[row(slots), row(slots), row(MLA_HEADS * MLA_V), row(n_conv)],
        out_shape=[jax.ShapeDtypeStruct((b, l, slots), BF16), jax.ShapeDtypeStruct((b, l, slots), BF16),
                   jax.ShapeDtypeStruct((b, l, MLA_HEADS * MLA_V), BF16),
                   jax.ShapeDtypeStruct((b, l, n_conv), F32)],
        compiler_params=_params("parallel", "parallel"),
        name="mla_prep",
    )(h, mod, *consts, cos, sin)


def _attn_kernel(q_ref, k_ref, v_ref, o_ref, *, lc, tq):
    def run(klen):
        lane = lax.broadcasted_iota(jnp.int32, (1, LANES), 1)
        for hp in range(MLA_HEADS // 2):
            vp = v_ref[0, 0:klen, hp * LANES:(hp + 1) * LANES]
            outs = []
            for hh in range(2):
                hs = slice((2 * hp + hh) * LANES, (2 * hp + hh + 1) * LANES)
                s = lax.dot_general(q_ref[0, :, hs], k_ref[0, 0:klen, hs], NT_DIMS, preferred_element_type=F32)
                p = jnp.exp(s - jnp.max(s, axis=-1, keepdims=True))
                l = jnp.sum(p, axis=-1, keepdims=True)
                outs.append(jnp.dot(p.astype(BF16), vp, preferred_element_type=F32) / l)
            o_ref[0, :, hp * LANES:(hp + 1) * LANES] = jnp.where(lane < MLA_V, outs[0], outs[1]).astype(BF16)

    is_ctx = pl.program_id(1) * tq < lc

    @pl.when(is_ctx)
    def _():
        run(lc)

    @pl.when(jnp.logical_not(is_ctx))
    def _():
        run(k_ref.shape[1])


def attention(q, k, v, lc):
    b, l, slots = q.shape
    dv = v.shape[-1]
    tq = _pick(lc, (256, 128, 64, 32, 16, 8))
    return pl.pallas_call(
        functools.partial(_attn_kernel, lc=lc, tq=tq),
        grid=(b, l // tq),
        in_specs=[pl.BlockSpec((1, tq, slots), lambda bi, i: (bi, i, 0)),
                  pl.BlockSpec((1, l, slots), lambda bi, i: (bi, 0, 0)),
                  pl.BlockSpec((1, l, dv), lambda bi, i: (bi, 0, 0))],
        out_specs=pl.BlockSpec((1, tq, dv), lambda bi, i: (bi, i, 0)),
        out_shape=jax.ShapeDtypeStruct((b, l, dv), BF16),
        compiler_params=_params("parallel", "parallel"),
        name="attention",
    )(q, k, v)


CONV_ROWS = 64
CONV_HALO = 16


def _conv_kernel(u_ref, w_ref, b_ref, g_ref, beta_ref, o_ref, pad_ref, y_ref, *, segments, ch):
    half = CONV_WIDTH // 2
    win = CONV_ROWS + 2 * CONV_HALO
    nlb = ch // LANES
    for seg_start, seg_len in segments:
        zeros = jnp.zeros((CONV_HALO, ch), F32)
        pad_ref[0:CONV_HALO, :] = zeros
        pad_ref[CONV_HALO + seg_len:2 * CONV_HALO + seg_len, :] = zeros

        def glu_body(i, carry):
            r0 = pl.multiple_of(i * CONV_ROWS, CONV_ROWS)
            u = u_ref[0, pl.ds(seg_start + r0, CONV_ROWS), :]
            pad_ref[pl.ds(CONV_HALO + r0, CONV_ROWS), :] = u[:, :ch] * _sigmoid(u[:, ch:])
            return carry

        lax.fori_loop(0, seg_len // CONV_ROWS, glu_body, 0)

        def conv_body(i, carry):
            r0 = pl.multiple_of(i * CONV_ROWS, CONV_ROWS)
            for lb in range(nlb):
                ls = slice(lb * LANES, (lb + 1) * LANES)
                xwin = pad_ref[pl.ds(r0, win), ls]
                acc = jnp.zeros((CONV_ROWS, LANES), F32)
                for j in range(CONV_WIDTH):
                    off = CONV_HALO - half + j
                    shifted = pltpu.roll(xwin, shift=(win - off) % win, axis=0)[0:CONV_ROWS]
                    acc = acc + shifted * w_ref[j:j + 1, ls]
                y_ref[pl.ds(seg_start + r0, CONV_ROWS), ls] = acc + b_ref[0:1, ls]
            return carry

        lax.fori_loop(0, seg_len // CONV_ROWS, conv_body, 0)

    def ln_body(i, carry):
        r0 = pl.multiple_of(i * CONV_ROWS, CONV_ROWS)
        y = y_ref[pl.ds(r0, CONV_ROWS), :]
        mu = jnp.mean(y, axis=-1, keepdims=True)
        d = y - mu
        var = jnp.mean(d * d, axis=-1, keepdims=True)
        z = d * lax.rsqrt(var + CONV_LN_EPS) * g_ref[0:1, :] + beta_ref[0:1, :]
        o_ref[0, pl.ds(r0, CONV_ROWS), :] = (z * _sigmoid(z)).astype(o_ref.dtype)
        return carry

    total = sum(s[1] for s in segments)
    lax.fori_loop(0, total // CONV_ROWS, ln_body, 0)


def conformer_conv(u, conv_w, conv_b, ln_g, ln_b, segments):
    b, l, ch2 = u.shape
    ch = ch2 // 2
    max_seg = max(s[1] for s in segments)
    row = lambda t: t.reshape(1, ch)
    return pl.pallas_call(
        functools.partial(_conv_kernel, segments=segments, ch=ch),
        grid=(b,),
        in_specs=[pl.BlockSpec((1, l, ch2), lambda i: (i, 0, 0)),
                  pl.BlockSpec((CONV_WIDTH, ch), lambda i: (0, 0)),
                  pl.BlockSpec((1, ch), lambda i: (0, 0)),
                  pl.BlockSpec((1, ch), lambda i: (0, 0)),
                  pl.BlockSpec((1, ch), lambda i: (0, 0))],
        out_specs=pl.BlockSpec((1, l, ch), lambda i: (i, 0, 0)),
        out_shape=jax.ShapeDtypeStruct((b, l, ch), BF16),
        scratch_shapes=[pltpu.VMEM((max_seg + 2 * CONV_HALO, ch), F32),
                        pltpu.VMEM((l, ch), F32)],
        compiler_params=_params("parallel"),
        name="conformer_conv",
    )(u, conv_w, row(conv_b), row(ln_g), row(ln_b))


def _rw_feat_kernel(*refs, lc, l, tm, has_vres):
    (h_ref, hp_ref, hn_ref, mod_ref, gn_ref, mu_ref, wr_ref, wk_ref, wv_ref, w1_ref, w2_ref, a1_ref, a2_ref,
     g1_ref, g2_ref, w0_ref, a0_ref, kk_gain_ref, ka_ref, ones_ref) = refs[:20]
    rest = refs[20:]
    if has_vres:
        vf_ref, v0_ref, v1_ref, v2_ref = rest[:4]
        rest = rest[4:]
    lw_ref, kd_ref, bd_ref, r_ref, v_ref, kk_ref, g_ref = rest
    d = h_ref.shape[-1]
    first = pl.program_id(1) * tm
    n_ext = tm + 2 * SUBLANES
    sel = _row_select(mod_ref, first - SUBLANES, n_ext, lc)
    h_ext = jnp.concatenate([hp_ref[0], h_ref[0], hn_ref[0]], axis=0)
    a_ext = _norm_mod(h_ext, gn_ref[...], sel(1), sel(0))
    cur = a_ext[SUBLANES:SUBLANES + tm]
    pos = first + lax.broadcasted_iota(jnp.int32, (tm, 1), 0)
    prev = pltpu.roll(a_ext, 1, axis=0)[SUBLANES:SUBLANES + tm]
    nxt = pltpu.roll(a_ext, n_ext - 1, axis=0)[SUBLANES:SUBLANES + tm]
    d_prev = jnp.where((pos == 0) | (pos == lc), 0.0, prev) - cur
    d_next = jnp.where((pos == lc - 1) | (pos == l - 1), 0.0, nxt) - cur
    mix = lambda i: (cur + d_prev * mu_ref[0, i:i + 1, :] + d_next * mu_ref[1, i:i + 1, :]).astype(BF16)

    xv = mix(3)
    k = _bdot(mix(2), wk_ref[...])
    v = _bdot(xv, wv_ref[...])
    if has_vres:
        gate = _sigmoid(v0_ref[...] + _bdot(_bdot(xv, v1_ref[...]), v2_ref[...]))
        v = v + (vf_ref[0] - v) * gate
    v_ref[0] = v
    r_ref[0] = _bdot(mix(0), wr_ref[...])
    g_ref[0] = _bdot(_sigmoid(_bdot(mix(5), g1_ref[...])), g2_ref[...])

    kkf = k * kk_gain_ref[...]
    ss = _bdot(kkf * kkf, ones_ref[...])
    kk = kkf / jnp.maximum(jnp.sqrt(ss), 1e-12)
    kk_ref[0] = kk
    w_lora = _bdot(jnp.tanh(_bdot(mix(1), w1_ref[...])), w2_ref[...])
    a_lora = _bdot(_bdot(mix(4), a1_ref[...]), a2_ref[...])
    for di in range(2):
        z = -(w0_ref[di:di + 1, :] + w_lora[:, di * d:(di + 1) * d])
        softplus = jnp.maximum(z, 0.0) + jnp.log(1.0 + jnp.exp(-jnp.abs(z)))
        lw_ref[di, 0] = -jnp.exp(-softplus - 0.5)
        a_rate = _sigmoid(a0_ref[di:di + 1, :] + a_lora[:, di * d:(di + 1) * d])
        kd_ref[di, 0] = k * (1 + (a_rate - 1) * ka_ref[...])
        bd_ref[di, 0] = kk * a_rate


def _pad_to(w, rows, cols):
    return jnp.pad(w, ((0, rows - w.shape[0]), (0, cols - w.shape[1])))


def _head_ones(d):
    hid = jnp.arange(d) // RW_HEAD
    return (hid[:, None] == hid[None, :]).astype(BF16)


def rw_features(h, mod, gn, p, v_first, lc):
    b, l, d = h.shape
    tm = _pick(lc, (256, 128, 64, 32, 16, 8))
    nblk = l // SUBLANES
    tb = tm // SUBLANES
    bf = lambda t: t.astype(BF16)
    cat2 = lambda t: jnp.concatenate([t[0], t[1]], axis=1)
    bdiag = lambda t: jnp.concatenate(
        [jnp.concatenate([t[0], jnp.zeros_like(t[0])], axis=1),
         jnp.concatenate([jnp.zeros_like(t[1]), t[1]], axis=1)], axis=0)
    lane_pad = lambda n: -(-n // LANES) * LANES
    rg = lane_pad(p['g1'].shape[1])
    consts = [gn.reshape(1, d), p['mu'], bf(p['wr']), bf(p['wk']), bf(p['wv']),
              bf(cat2(p['w1'])), bf(bdiag(p['w2'])), bf(cat2(p['a1'])), bf(bdiag(p['a2'])),
              bf(_pad_to(p['g1'], d, rg)), bf(_pad_to(p['g2'], rg, d)),
              p['w0'], p['a0'], p['k_k'].reshape(1, d), p['k_a'].reshape(1, d), _head_ones(d)]
    row = pl.BlockSpec((1, tm, d), lambda bi, j: (bi, j, 0))
    row2 = pl.BlockSpec((2, 1, tm, d), lambda bi, j: (0, bi, j, 0))
    halo_prev = pl.BlockSpec((1, SUBLANES, d), lambda bi, j: (bi, jnp.maximum(j * tb - 1, 0), 0))
    halo_next = pl.BlockSpec((1, SUBLANES, d), lambda bi, j: (bi, jnp.minimum((j + 1) * tb, nblk - 1), 0))
    in_specs = [row, halo_prev, halo_next, pl.BlockSpec((1, 2, 6, d), lambda bi, j: (bi, 0, 0, 0))]
    in_specs += [_const_spec(t.shape) for t in consts]
    args = [h, h, h, mod] + consts
    has_vres = v_first is not None
    if has_vres:
        rv = lane_pad(p['v1'].shape[1])
        extra = [p['v0'].reshape(1, d), bf(_pad_to(p['v1'], d, rv)), bf(_pad_to(p['v2'], rv, d))]
        in_specs += [row] + [_const_spec(t.shape) for t in extra]
        args += [v_first] + extra
    one = jax.ShapeDtypeStruct((b, l, d), F32)
    two = jax.ShapeDtypeStruct((2, b, l, d), F32)
    return pl.pallas_call(
        functools.partial(_rw_feat_kernel, lc=lc, l=l, tm=tm, has_vres=has_vres),
        grid=(b, l // tm),
        in_specs=in_specs,
        out_specs=[row2, row2, row2, row, row, row, row],
        out_shape=[two, two, two, one, one, one, one],
        compiler_params=_params("parallel", "parallel"),
        name="rw_features",
    )(*args)


def _tile_rows(x, mask):
    return jnp.concatenate([x.astype(BF16)] * SCAN_G, axis=0) * mask


def _scan_kernel(lw_ref, kd_ref, bd_ref, r_ref, v_ref, kk_ref, cum_ref, strict_ref, incl_ref, lvl_ref,
                 blk_ref, y_ref, state_ref):
    c = SCAN_C
    @pl.when(pl.program_id(2) == 0)
    def _():
        state_ref[...] = jnp.zeros_like(state_ref)

    cum2 = cum_ref[0]
    strict = strict_ref[0]
    incl = incl_ref[0]
    blk = blk_ref[...]
    groups = range(lw_ref.shape[-1] // SCAN_W)
    ls = lambda g: slice(g * SCAN_W, (g + 1) * SCAN_W)
    bf = lambda t: t.astype(BF16)

    lw = [lw_ref[0, 0, :, ls(g)] for g in groups]
    lp = []
    for g in groups:
        hi = bf(lw[g])
        lo = bf(lw[g] - hi.astype(F32))
        lp.append(_bdot(cum2, jnp.concatenate([hi, lo], axis=0)))

    total, ar, bk, bk_tail = [], [], [], []
    for g in groups:
        tot = jnp.sum(lw[g], axis=0, keepdims=True)
        inv = jnp.exp(-lp[g])
        tail = jnp.exp(tot - lp[g])
        kd = kd_ref[0, 0, :, ls(g)]
        bd = bd_ref[0, 0, :, ls(g)]
        at = -kk_ref[0, :, ls(g)] * jnp.exp(lp[g] - lw[g])
        rt = r_ref[0, :, ls(g)] * jnp.exp(lp[g])
        total.append(tot)
        ar.append(bf(jnp.concatenate([at, rt], axis=0)))
        bk.append(jnp.concatenate([_tile_rows(bd * inv, blk), _tile_rows(kd * inv, blk)], axis=0))
        bk_tail.append(bf(jnp.concatenate([bd * tail, kd * tail], axis=0)))

    gg = [_bdot(ar[g], bk[g], NT_DIMS) for g in groups]
    sh = [_bdot(ar[g], bf(state_ref[g]), NT_DIMS) for g in groups]

    gc = SCAN_G * c
    vext = [_tile_rows(v_ref[0, :, ls(g)], blk) for g in groups]
    rhs = [sh[g][:c] + _bdot(bf(gg[g][:c, gc:] * strict), vext[g]) for g in groups]

    m = [gg[g][:c, :gc] * strict for g in groups]
    for lv in range(lvl_ref.shape[1]):
        m = [m[g] + _bdot(bf(m[g] * lvl_ref[0, lv]), _tile_rows(m[g], blk)) for g in groups]
    u = [rhs[g] + _bdot(bf(m[g]), _tile_rows(rhs[g], blk)) for g in groups]

    for g in groups:
        rbk = bf(gg[g][c:] * jnp.concatenate([incl, incl], axis=1))
        y = sh[g][c:] + _bdot(rbk, jnp.concatenate([_tile_rows(u[g], blk), vext[g]], axis=0))
        y_ref[0, 0, :, ls(g)] = y
    for g in groups:
        uv = bf(jnp.concatenate([u[g], v_ref[0, :, ls(g)]], axis=0))
        upd = _bdot(uv, bk_tail[g], TN_DIMS)
        state_ref[g] = (state_ref[g] * jnp.exp(total[g]) + upd) * blk.astype(F32)


def _scan_masks():
    c, g = SCAN_C, SCAN_G
    t = jnp.arange(c)[:, None]
    s = jnp.arange(c)[None, :]
    cum = jnp.stack([s <= t, s >= t])
    cum = jnp.concatenate([cum, cum], axis=2).astype(BF16)
    sg = jnp.arange(g * c)[None, :] % c
    strict = jnp.stack([sg < t, sg > t]).astype(F32)
    incl = jnp.stack([sg <= t, sg >= t]).astype(F32)
    sizes = [1 << i for i in range(c.bit_length() - 1)]
    lvl = jnp.stack([((t // (2 * s) == sg // (2 * s)) & (t // s != sg // s)).astype(F32) for s in sizes])
    lvl = strict[:, None] * lvl[None]
    rb = jnp.arange(SCAN_W)[:, None] // RW_HEAD
    cb = jnp.arange(SCAN_W)[None, :] // RW_HEAD
    blk = (rb == cb).astype(BF16)
    return cum, strict, incl, lvl, blk


def wkv_scan(lw, kd, bd, r, v, kk, lc):
    _, b, l, d = lw.shape
    c = SCAN_C
    nc = l // c
    ncc = lc // c
    cum, strict, incl, lvl, blk = _scan_masks()
    nlv = lvl.shape[1]

    def chunk(di, ci):
        rev = jnp.where(ci < ncc, ncc - 1 - ci, nc - 1 + ncc - ci)
        return jnp.where(di == 0, ci, rev)

    dspec = pl.BlockSpec((1, 1, c, d), lambda bi, di, ci: (di, bi, chunk(di, ci), 0))
    sspec = pl.BlockSpec((1, c, d), lambda bi, di, ci: (bi, chunk(di, ci), 0))
    return pl.pallas_call(
        _scan_kernel,
        grid=(b, 2, nc),
        in_specs=[dspec, dspec, dspec, sspec, sspec, sspec,
                  pl.BlockSpec((1, c, 2 * c), lambda bi, di, ci: (di, 0, 0)),
                  pl.BlockSpec((1, c, SCAN_G * c), lambda bi, di, ci: (di, 0, 0)),
                  pl.BlockSpec((1, c, SCAN_G * c), lambda bi, di, ci: (di, 0, 0)),
                  pl.BlockSpec((1, nlv, c, SCAN_G * c), lambda bi, di, ci: (di, 0, 0, 0)),
                  pl.BlockSpec((SCAN_W, SCAN_W), lambda bi, di, ci: (0, 0))],
        out_specs=dspec,
        out_shape=jax.ShapeDtypeStruct((2, b, l, d), F32),
        scratch_shapes=[pltpu.VMEM((d // SCAN_W, SCAN_W, SCAN_W), F32)],
        compiler_params=_params("parallel", "parallel", "arbitrary"),
        name="wkv_scan",
    )(lw, kd, bd, r, v, kk, cum, strict, incl, lvl, blk)


def _rw_readout_kernel(y_ref, r_ref, v_ref, g_ref, kd_ref, rk_ref, lng_ref, lnb_ref, ones_ref, o_ref):
    ones = ones_ref[...]
    inv_n = 1.0 / RW_HEAD
    y = y_ref[0, 0] + y_ref[1, 0]
    mu = _dot2(y, ones) * inv_n
    dy = y - mu
    var = _dot2(dy * dy, ones) * inv_n
    yn = dy * lax.rsqrt(var + RW_GN_EPS) * lng_ref[...] + lnb_ref[...]
    r = r_ref[0]
    kmix = kd_ref[0, 0] * rk_ref[0:1, :] + kd_ref[1, 0] * rk_ref[1:2, :]
    coef = _dot2(r * kmix, ones)
    o_ref[0] = ((yn + coef * v_ref[0]) * g_ref[0]).astype(BF16)


def rw_readout(y2, r, v, g, kd, p):
    b, l, d = r.shape
    tm = _pick(l, (384, 256, 128, 64, 32, 16, 8))
    row = pl.BlockSpec((1, tm, d), lambda bi, j: (bi, j, 0))
    row2 = pl.BlockSpec((2, 1, tm, d), lambda bi, j: (0, bi, j, 0))
    consts = [p['r_k'].reshape(2, d), p['ln_g'].reshape(1, d), p['ln_b'].reshape(1, d), _head_ones(d)]
    return pl.pallas_call(
        _rw_readout_kernel,
        grid=(b, l // tm),
        in_specs=[row2, row, row, row, row2] + [_const_spec(t.shape) for t in consts],
        out_specs=row,
        out_shape=jax.ShapeDtypeStruct((b, l, d), BF16),
        compiler_params=_params("parallel", "parallel"),
        name="rw_readout",
    )(y2, r, v, g, kd, *consts)


def kernel(x, c, ctx, c_ctx, ada_w, ada_b, norm_mix, norm_ffn, ffn_w1, ffn_w3, ffn_w2, even_w_in, mla_q_norm, mla_wq_b, mla_kv_norm, mla_wkv_b, mla_q_qk, mla_k_qk, conv_w, conv_b, conv_ln_g, conv_ln_b, even_w_out, rw_mu, rw_wr, rw_wk, rw_wv, rw_w0, rw_w1, rw_w2, rw_a0, rw_a1, rw_a2, rw_v0, rw_v1, rw_v2, rw_k_k, rw_k_a, rw_r_k, rw_g1, rw_g2, rw_ln_g, rw_ln_b, rw_wo):
    b, ll, d = x.shape
    lc = ctx.shape[1]
    depth = ada_w.shape[0]
    bf = lambda t: t.astype(BF16)
    h = jnp.concatenate([ctx, x], axis=1)
    tables = _rope_slot_tables(lc, ll)

    cond = jnp.concatenate([c, c_ctx[None, :]], axis=0)
    cond = jnp.pad(cond, ((0, -(-(b + 1) // SUBLANES) * SUBLANES - (b + 1)), (0, 0)))
    v_first = None
    for i in range(depth):
        j = i // 2
        m = mm(cond, ada_w[i], pre="silu")[:b + 1] + ada_b[i]
        m_lat = m[:b].reshape(b, 1, 6, d)
        m_ctx = jnp.broadcast_to(m[b].reshape(1, 1, 6, d), (b, 1, 6, d))
        mod = jnp.concatenate([m_ctx, m_lat], axis=1)
        if i % 2 == 0:
            p = {'w_in': even_w_in[j], 'q_norm': mla_q_norm[j], 'wq_b': mla_wq_b[j],
                 'kv_norm': mla_kv_norm[j], 'wkv_b': mla_wkv_b[j], 'q_qk': mla_q_qk[j], 'k_qk': mla_k_qk[j]}
            q, k, v, zconv = mla_prep(h, mod, norm_mix[i], p, tables, lc)
            o = attention(q, k, v, lc)
            cv = conformer_conv(zconv, conv_w[j], conv_b[j], conv_ln_g[j], conv_ln_b[j], ((0, lc), (lc, ll)))
            n_attn = o.shape[-1]
            xs = [o, cv]
            wps = [bf(even_w_out[j][:n_attn]), bf(even_w_out[j][n_attn:])]
        else:
            p = {'mu': rw_mu[j], 'wr': rw_wr[j], 'wk': rw_wk[j], 'wv': rw_wv[j],
                 'w0': rw_w0[j], 'w1': rw_w1[j], 'w2': rw_w2[j],
                 'a0': rw_a0[j], 'a1': rw_a1[j], 'a2': rw_a2[j],
                 'k_k': rw_k_k[j], 'k_a': rw_k_a[j], 'r_k': rw_r_k[j],
                 'g1': rw_g1[j], 'g2': rw_g2[j], 'ln_g': rw_ln_g[j], 'ln_b': rw_ln_b[j]}
            if j > 0:
                p['v0'] = rw_v0[j - 1]
                p['v1'] = rw_v1[j - 1]
                p['v2'] = rw_v2[j - 1]
            lw, kd, bd, r, v, kk, g = rw_features(h, mod, norm_mix[i], p, v_first if j > 0 else None, lc)
            if v_first is None:
                v_first = v
            y2 = wkv_scan(lw, kd, bd, r, v, kk, lc)
            xs = [rw_readout(y2, r, v, g, kd, p)]
            wps = [bf(rw_wo[j])]
        h = proj_ffn(h, xs, wps, mod, norm_ffn[i], bf(ffn_w1[i]), bf(ffn_w3[i]), bf(ffn_w2[i]), lc)
    return h[:, lc:]
```

```python
import functools
import math

import jax
import jax.numpy as jnp
from jax import lax
from jax.experimental import pallas as pl
from jax.experimental.pallas import tpu as pltpu

F32 = jnp.float32
BF16 = jnp.bfloat16

EPS = 1e-6
GRID_W = 64
ROPE_THETA = 10000.0
MLA_HEADS = 8
MLA_NOPE = 64
MLA_ROPE = 32
MLA_QK = MLA_NOPE + MLA_ROPE
MLA_V = 64
AXIS_DIM = MLA_ROPE // 2
CONV_WIDTH = 31
CONV_LN_EPS = 1e-5
RW_HEAD = 64
RW_GN_EPS = 6.4e-4
DECAY_SCALE = -math.exp(-0.5)

LANES = 128
SUBLANES = 8
VMEM_LIMIT = 56 * 1024 * 1024

SCAN_C = 64
SCAN_G = 4
SCAN_W = SCAN_G * RW_HEAD

NT_DIMS = (((1,), (1,)), ((), ()))
TN_DIMS = (((0,), (0,)), ((), ()))


def _pick(n, cands):
    for c in cands:
        if c <= n and n % c == 0:
            return c
    return n


def _const_spec(shape):
    nd = len(shape)
    return pl.BlockSpec(shape, lambda *_: (0,) * nd, pipeline_mode=pl.Buffered(1))


def _params(*sem):
    return pltpu.CompilerParams(dimension_semantics=sem, vmem_limit_bytes=VMEM_LIMIT)


def _bdot(a, b, dims=None):
    a = a.astype(BF16)
    b = b.astype(BF16)
    if dims is None:
        return jnp.dot(a, b, preferred_element_type=F32)
    return lax.dot_general(a, b, dims, preferred_element_type=F32)


def _dot2(x, w):
    hi = x.astype(BF16)
    lo = (x - hi.astype(F32)).astype(BF16)
    return _bdot(hi, w) + _bdot(lo, w)


def _row_select(mod_ref, first_row, n, lc):
    rows = first_row + lax.broadcasted_iota(jnp.int32, (n, 1), 0)
    is_ctx = rows < lc
    return lambda i: jnp.where(is_ctx, mod_ref[0, 0, i:i + 1, :], mod_ref[0, 1, i:i + 1, :])


def _norm_mod(h, gain, scale, shift):
    return h * lax.rsqrt(jnp.mean(h * h, axis=-1, keepdims=True) + EPS) * gain * (1 + scale) + shift


def _sigmoid(x):
    return 0.5 * jnp.tanh(0.5 * x) + 0.5


def _mm_kernel(x_ref, w_ref, o_ref, *, pre):
    x = x_ref[...]
    if pre == "silu":
        x = x * _sigmoid(x)
    o_ref[...] = jnp.dot(x.astype(BF16), w_ref[...].astype(BF16), preferred_element_type=F32)


def mm(x, w, pre=None):
    m, k = x.shape
    n = w.shape[1]
    tm = _pick(m, (512, 384, 256, 128, 64, 32, 16, 8))
    tn = n if n <= 1024 else _pick(n, (1024, 896, 768, 640, 512, 384, 256, 128))
    return pl.pallas_call(
        functools.partial(_mm_kernel, pre=pre),
        grid=(m // tm, n // tn),
        in_specs=[pl.BlockSpec((tm, k), lambda i, j: (i, 0)),
                  pl.BlockSpec((k, tn), lambda i, j: (0, j))],
        out_specs=pl.BlockSpec((tm, tn), lambda i, j: (i, j)),
        out_shape=jax.ShapeDtypeStruct((m, n), F32),
        compiler_params=_params("parallel", "parallel"),
        name="mm",
    )(x, w)


def _proj_ffn_kernel(*refs, nx, lc, tm, hid_chunk):
    h_ref, mod_ref, gn_ref = refs[0], refs[1], refs[2]
    x_refs = refs[3:3 + nx]
    wp_refs = refs[3 + nx:3 + 2 * nx]
    w1_ref, w3_ref, w2_ref, o_ref = refs[3 + 2 * nx:]
    sel = _row_select(mod_ref, pl.program_id(1) * tm, tm, lc)
    proj = _bdot(x_refs[0][0], wp_refs[0][...])
    for i in range(1, nx):
        proj = proj + _bdot(x_refs[i][0], wp_refs[i][...])
    h1 = h_ref[0] + sel(2) * proj
    f = _norm_mod(h1, gn_ref[...], sel(4), sel(3)).astype(BF16)
    acc = None
    for c0 in range(0, w1_ref.shape[1], hid_chunk):
        a = _bdot(f, w1_ref[:, c0:c0 + hid_chunk])
        b = _bdot(f, w3_ref[:, c0:c0 + hid_chunk])
        part = _bdot(a * _sigmoid(a) * b, w2_ref[c0:c0 + hid_chunk, :])
        acc = part if acc is None else acc + part
    o_ref[0] = h1 + sel(5) * acc


def proj_ffn(h, xs, wps, mod, gn, w1, w3, w2, lc):
    b, l, d = h.shape
    tm = _pick(l, (384, 256, 128, 64, 32, 16, 8))
    nh = w1.shape[1]
    hid_chunk = nh // 2 if (nh // 2) % LANES == 0 else nh
    row = lambda t: pl.BlockSpec((1, tm, t.shape[-1]), lambda bi, j: (bi, j, 0))
    return pl.pallas_call(
        functools.partial(_proj_ffn_kernel, nx=len(xs), lc=lc, tm=tm, hid_chunk=hid_chunk),
        grid=(b, l // tm),
        in_specs=[row(h), pl.BlockSpec((1, 2, 6, d), lambda bi, j: (bi, 0, 0, 0)), _const_spec((1, d))]
                 + [row(x) for x in xs] + [_const_spec(w.shape) for w in wps]
                 + [_const_spec(w1.shape), _const_spec(w3.shape), _const_spec(w2.shape)],
        out_specs=row(h),
        out_shape=jax.ShapeDtypeStruct(h.shape, F32),
        compiler_params=_params("parallel", "parallel"),
        name="proj_ffn",
    )(h, mod, gn.reshape(1, d), *xs, *wps, w1, w3, w2)


def _mla_prep_kernel(h_ref, mod_ref, gn_ref, win_ref, qn_ref, kvn_ref, wq_ref, wk_ref, wv_ref,
                     qg_ref, kg_ref, ones_ref, cos_ref, sin_ref,
                     q_ref, k_ref, v_ref, zc_ref, *, lc, tm, q_rank, kv_rank, scale):
    sel = _row_select(mod_ref, pl.program_id(1) * tm, tm, lc)
    a = _norm_mod(h_ref[0], gn_ref[...], sel(1), sel(0))
    z = _bdot(a, win_ref[...])
    rms = lambda t, g: t * lax.rsqrt(jnp.mean(t * t, axis=-1, keepdims=True) + EPS) * g
    zq = rms(z[:, :q_rank], qn_ref[...]).astype(BF16)
    zkv = rms(z[:, q_rank:q_rank + kv_rank], kvn_ref[...]).astype(BF16)
    slots = wq_ref.shape[1]
    k0 = q_rank + kv_rank
    q = _bdot(zq, wq_ref[...])
    k = _bdot(zkv, wk_ref[...]) + z[:, k0:k0 + slots]
    v_ref[0] = _bdot(zkv, wv_ref[...]).astype(BF16)
    zc_ref[0] = z[:, k0 + slots:]

    lane = lax.broadcasted_iota(jnp.int32, (1, slots), 1) % LANES
    first_half = ((lane - MLA_NOPE) % AXIS_DIM) < AXIS_DIM // 2
    cos = cos_ref[...]
    sin = sin_ref[...]

    def head_norm_rope(t, gain):
        ss = _bdot(t * t, ones_ref[...]) * (1.0 / MLA_QK)
        t = t * lax.rsqrt(ss + EPS) * gain
        partner = jnp.where(first_half, pltpu.roll(t, slots - AXIS_DIM // 2, axis=1),
                            pltpu.roll(t, AXIS_DIM // 2, axis=1))
        return t * cos + partner * sin

    q_ref[0] = (head_norm_rope(q, qg_ref[...]) * scale).astype(BF16)
    k_ref[0] = head_norm_rope(k, kg_ref[...]).astype(BF16)


def _head_slots(w, width):
    kdim = w.shape[0]
    w = w.reshape(kdim, MLA_HEADS, width)
    return jnp.pad(w, ((0, 0), (0, 0), (0, LANES - width))).reshape(kdim, MLA_HEADS * LANES)


def _rope_slot_tables(lc, ll):
    rows = ll // GRID_W
    row = jnp.repeat(jnp.arange(rows, dtype=F32), GRID_W)
    col = jnp.tile(jnp.arange(GRID_W, dtype=F32), rows)
    inv = ROPE_THETA ** (-jnp.arange(0, AXIS_DIM, 2, dtype=F32) / AXIS_DIM)
    cos_parts, sin_parts = [], []
    for pos in (row, col):
        ang = pos[:, None] * inv
        cos_parts += [jnp.cos(ang), jnp.cos(ang)]
        sin_parts += [-jnp.sin(ang), jnp.sin(ang)]
    ones = jnp.ones((ll, MLA_NOPE), F32)
    zeros = jnp.zeros((ll, MLA_NOPE), F32)
    tail1 = jnp.ones((ll, LANES - MLA_QK), F32)
    tail0 = jnp.zeros((ll, LANES - MLA_QK), F32)
    cos = jnp.concatenate([ones] + cos_parts + [tail1], axis=1)
    sin = jnp.concatenate([zeros] + sin_parts + [tail0], axis=1)
    cos = jnp.concatenate([jnp.ones((lc, LANES), F32), cos], axis=0)
    sin = jnp.concatenate([jnp.zeros((lc, LANES), F32), sin], axis=0)
    return jnp.tile(cos, (1, MLA_HEADS)), jnp.tile(sin, (1, MLA_HEADS))


def mla_prep(h, mod, gn, p, tables, lc):
    b, l, d = h.shape
    q_rank = p['q_norm'].shape[0]
    kv_rank = p['kv_norm'].shape[0]
    slots = MLA_HEADS * LANES
    w_in = p['w_in']
    k0 = q_rank + kv_rank
    w_rope = jnp.tile(jnp.pad(w_in[:, k0:k0 + MLA_ROPE], ((0, 0), (MLA_NOPE, LANES - MLA_QK))), (1, MLA_HEADS))
    w_ext = jnp.concatenate([w_in[:, :k0], w_rope, w_in[:, k0 + MLA_ROPE:]], axis=1).astype(BF16)
    n_conv = w_in.shape[1] - k0 - MLA_ROPE
    wkv = p['wkv_b'].reshape(kv_rank, MLA_HEADS, MLA_NOPE + MLA_V)
    wq = _head_slots(p['wq_b'], MLA_QK).astype(BF16)
    wk = _head_slots(wkv[:, :, :MLA_NOPE].reshape(kv_rank, -1), MLA_NOPE).astype(BF16)
    wv = wkv[:, :, MLA_NOPE:].reshape(kv_rank, MLA_HEADS * MLA_V).astype(BF16)
    slot_gain = lambda g: jnp.tile(jnp.pad(g, (0, LANES - MLA_QK)), MLA_HEADS).reshape(1, slots)
    sid = jnp.arange(slots) // LANES
    ones = (sid[:, None] == sid[None, :]).astype(BF16)
    tm = _pick(l, (384, 256, 128, 64, 32, 16, 8))
    cos, sin = tables
    row = lambda n: pl.BlockSpec((1, tm, n), lambda bi, j: (bi, j, 0))
    tab = pl.BlockSpec((tm, slots), lambda bi, j: (j, 0))
    consts = [gn.reshape(1, d), w_ext, p['q_norm'].reshape(1, -1), p['kv_norm'].reshape(1, -1), wq, wk, wv,
              slot_gain(p['q_qk']), slot_gain(p['k_qk']), ones]
    return pl.pallas_call(
        functools.partial(_mla_prep_kernel, lc=lc, tm=tm, q_rank=q_rank, kv_rank=kv_rank, scale=MLA_QK ** -0.5),
        grid=(b, l // tm),
        in_specs=[row(d), pl.BlockSpec((1, 2, 6, d), lambda bi, j: (bi, 0, 0, 0))]
                 + [_const_spec(t.shape) for t in consts] + [tab, tab],
        out_specs=[row(slots), row(slots), row(MLA_HEADS * MLA_V), row(n_conv)],
        out_shape=[jax.ShapeDtypeStruct((b, l, slots), BF16), jax.ShapeDtypeStruct((b, l, slots), BF16),
                   jax.ShapeDtypeStruct((b, l, MLA_HEADS * MLA_V), BF16),
                   jax.ShapeDtypeStruct((b, l, n_conv), F32)],
        compiler_params=_params("parallel", "parallel"),
        name="mla_prep",
    )(h, mod, *consts, cos, sin)


def _attn_kernel(q_ref, k_ref, v_ref, o_ref, *, lc, tq):
    def run(klen):
        lane = lax.broadcasted_iota(jnp.int32, (1, LANES), 1)
        for hp in range(MLA_HEADS // 2):
            vp = v_ref[0, 0:klen, hp * LANES:(hp + 1) * LANES]
            outs = []
            for hh in range(2):
                hs = slice((2 * hp + hh) * LANES, (2 * hp + hh + 1) * LANES)
                s = lax.dot_general(q_ref[0, :, hs], k_ref[0, 0:klen, hs], NT_DIMS, preferred_element_type=F32)
                p = jnp.exp(s - jnp.max(s, axis=-1, keepdims=True))
                l = jnp.sum(p, axis=-1, keepdims=True)
                outs.append(jnp.dot(p.astype(BF16), vp, preferred_element_type=F32) / l)
            o_ref[0, :, hp * LANES:(hp + 1) * LANES] = jnp.where(lane < MLA_V, outs[0], outs[1]).astype(BF16)

    is_ctx = pl.program_id(1) * tq < lc

    @pl.when(is_ctx)
    def _():
        run(lc)

    @pl.when(jnp.logical_not(is_ctx))
    def _():
        run(k_ref.shape[1])


def attention(q, k, v, lc):
    b, l, slots = q.shape
    dv = v.shape[-1]
    tq = _pick(lc, (256, 128, 64, 32, 16, 8))
    return pl.pallas_call(
        functools.partial(_attn_kernel, lc=lc, tq=tq),
        grid=(b, l // tq),
        in_specs=[pl.BlockSpec((1, tq, slots), lambda bi, i: (bi, i, 0)),
                  pl.BlockSpec((1, l, slots), lambda bi, i: (bi, 0, 0)),
                  pl.BlockSpec((1, l, dv), lambda bi, i: (bi, 0, 0))],
        out_specs=pl.BlockSpec((1, tq, dv), lambda bi, i: (bi, i, 0)),
        out_shape=jax.ShapeDtypeStruct((b, l, dv), BF16),
        compiler_params=_params("parallel", "parallel"),
        name="attention",
    )(q, k, v)


CONV_ROWS = 64
CONV_HALO = 16


def _conv_kernel(u_ref, w_ref, b_ref, g_ref, beta_ref, o_ref, pad_ref, y_ref, *, segments, ch):
    half = CONV_WIDTH // 2
    win = CONV_ROWS + 2 * CONV_HALO
    nlb = ch // LANES
    for seg_start, seg_len in segments:
        zeros = jnp.zeros((CONV_HALO, ch), F32)
        pad_ref[0:CONV_HALO, :] = zeros
        pad_ref[CONV_HALO + seg_len:2 * CONV_HALO + seg_len, :] = zeros

        def glu_body(i, carry):
            r0 = pl.multiple_of(i * CONV_ROWS, CONV_ROWS)
            u = u_ref[0, pl.ds(seg_start + r0, CONV_ROWS), :]
            pad_ref[pl.ds(CONV_HALO + r0, CONV_ROWS), :] = u[:, :ch] * _sigmoid(u[:, ch:])
            return carry

        lax.fori_loop(0, seg_len // CONV_ROWS, glu_body, 0)

        def conv_body(i, carry):
            r0 = pl.multiple_of(i * CONV_ROWS, CONV_ROWS)
            for lb in range(nlb):
                ls = slice(lb * LANES, (lb + 1) * LANES)
                xwin = pad_ref[pl.ds(r0, win), ls]
                acc = jnp.zeros((CONV_ROWS, LANES), F32)
                for j in range(CONV_WIDTH):
                    off = CONV_HALO - half + j
                    shifted = pltpu.roll(xwin, shift=(win - off) % win, axis=0)[0:CONV_ROWS]
                    acc = acc + shifted * w_ref[j:j + 1, ls]
                y_ref[pl.ds(seg_start + r0, CONV_ROWS), ls] = acc + b_ref[0:1, ls]
            return carry

        lax.fori_loop(0, seg_len // CONV_ROWS, conv_body, 0)

    def ln_body(i, carry):
        r0 = pl.multiple_of(i * CONV_ROWS, CONV_ROWS)
        y = y_ref[pl.ds(r0, CONV_ROWS), :]
        mu = jnp.mean(y, axis=-1, keepdims=True)
        d = y - mu
        var = jnp.mean(d * d, axis=-1, keepdims=True)
        z = d * lax.rsqrt(var + CONV_LN_EPS) * g_ref[0:1, :] + beta_ref[0:1, :]
        o_ref[0, pl.ds(r0, CONV_ROWS), :] = (z * _sigmoid(z)).astype(o_ref.dtype)
        return carry

    total = sum(s[1] for s in segments)
    lax.fori_loop(0, total // CONV_ROWS, ln_body, 0)


def conformer_conv(u, conv_w, conv_b, ln_g, ln_b, segments):
    b, l, ch2 = u.shape
    ch = ch2 // 2
    max_seg = max(s[1] for s in segments)
    row = lambda t: t.reshape(1, ch)
    return pl.pallas_call(
        functools.partial(_conv_kernel, segments=segments, ch=ch),
        grid=(b,),
        in_specs=[pl.BlockSpec((1, l, ch2), lambda i: (i, 0, 0)),
                  pl.BlockSpec((CONV_WIDTH, ch), lambda i: (0, 0)),
                  pl.BlockSpec((1, ch), lambda i: (0, 0)),
                  pl.BlockSpec((1, ch), lambda i: (0, 0)),
                  pl.BlockSpec((1, ch), lambda i: (0, 0))],
        out_specs=pl.BlockSpec((1, l, ch), lambda i: (i, 0, 0)),
        out_shape=jax.ShapeDtypeStruct((b, l, ch), BF16),
        scratch_shapes=[pltpu.VMEM((max_seg + 2 * CONV_HALO, ch), F32),
                        pltpu.VMEM((l, ch), F32)],
        compiler_params=_params("parallel"),
        name="conformer_conv",
    )(u, conv_w, row(conv_b), row(ln_g), row(ln_b))


def _rw_feat_kernel(*refs, lc, l, tm, has_vres):
    (h_ref, hp_ref, hn_ref, mod_ref, gn_ref, mu_ref, wr_ref, wk_ref, wv_ref, w1_ref, w2_ref, a1_ref, a2_ref,
     g1_ref, g2_ref, w0_ref, a0_ref, kk_gain_ref, ka_ref, ones_ref) = refs[:20]
    rest = refs[20:]
    if has_vres:
        vf_ref, v0_ref, v1_ref, v2_ref = rest[:4]
        rest = rest[4:]
    lw_ref, kd_ref, bd_ref, r_ref, v_ref, kk_ref, g_ref = rest
    d = h_ref.shape[-1]
    first = pl.program_id(1) * tm
    n_ext = tm + 2 * SUBLANES
    sel = _row_select(mod_ref, first - SUBLANES, n_ext, lc)
    h_ext = jnp.concatenate([hp_ref[0], h_ref[0], hn_ref[0]], axis=0)
    a_ext = _norm_mod(h_ext, gn_ref[...], sel(1), sel(0))
    cur = a_ext[SUBLANES:SUBLANES + tm]
    pos = first + lax.broadcasted_iota(jnp.int32, (tm, 1), 0)
    prev = pltpu.roll(a_ext, 1, axis=0)[SUBLANES:SUBLANES + tm]
    nxt = pltpu.roll(a_ext, n_ext - 1, axis=0)[SUBLANES:SUBLANES + tm]
    d_prev = jnp.where((pos == 0) | (pos == lc), 0.0, prev) - cur
    d_next = jnp.where((pos == lc - 1) | (pos == l - 1), 0.0, nxt) - cur
    mix = lambda i: (cur + d_prev * mu_ref[0, i:i + 1, :] + d_next * mu_ref[1, i:i + 1, :]).astype(BF16)

    xv = mix(3)
    k = _bdot(mix(2), wk_ref[...])
    v = _bdot(xv, wv_ref[...])
    if has_vres:
        gate = _sigmoid(v0_ref[...] + _bdot(_bdot(xv, v1_ref[...]), v2_ref[...]))
        v = v + (vf_ref[0] - v) * gate
    v_ref[0] = v
    r_ref[0] = _bdot(mix(0), wr_ref[...])
    g_ref[0] = _bdot(_sigmoid(_bdot(mix(5), g1_ref[...])), g2_ref[...])

    kkf = k * kk_gain_ref[...]
    ss = _bdot(kkf * kkf, ones_ref[...])
    kk = kkf * lax.rsqrt(jnp.maximum(ss, 1e-24))
    kk_ref[0] = kk
    w_lora = _bdot(jnp.tanh(_bdot(mix(1), w1_ref[...])), w2_ref[...])
    a_lora = _bdot(_bdot(mix(4), a1_ref[...]), a2_ref[...])
    for di in range(2):
        lw_ref[di, 0] = DECAY_SCALE * _sigmoid(w0_ref[di:di + 1, :] + w_lora[:, di * d:(di + 1) * d])
        a_rate = _sigmoid(a0_ref[di:di + 1, :] + a_lora[:, di * d:(di + 1) * d])
        kd_ref[di, 0] = k * (1 + (a_rate - 1) * ka_ref[...])
        bd_ref[di, 0] = kk * a_rate


def _pad_to(w, rows, cols):
    return jnp.pad(w, ((0, rows - w.shape[0]), (0, cols - w.shape[1])))


def _head_ones(d):
    hid = jnp.arange(d) // RW_HEAD
    return (hid[:, None] == hid[None, :]).astype(BF16)


def rw_features(h, mod, gn, p, v_first, lc):
    b, l, d = h.shape
    tm = _pick(lc, (256, 128, 64, 32, 16, 8))
    nblk = l // SUBLANES
    tb = tm // SUBLANES
    bf = lambda t: t.astype(BF16)
    cat2 = lambda t: jnp.concatenate([t[0], t[1]], axis=1)
    bdiag = lambda t: jnp.concatenate(
        [jnp.concatenate([t[0], jnp.zeros_like(t[0])], axis=1),
         jnp.concatenate([jnp.zeros_like(t[1]), t[1]], axis=1)], axis=0)
    lane_pad = lambda n: -(-n // LANES) * LANES
    rg = lane_pad(p['g1'].shape[1])
    consts = [gn.reshape(1, d), p['mu'], bf(p['wr']), bf(p['wk']), bf(p['wv']),
              bf(cat2(p['w1'])), bf(bdiag(p['w2'])), bf(cat2(p['a1'])), bf(bdiag(p['a2'])),
              bf(_pad_to(p['g1'], d, rg)), bf(_pad_to(p['g2'], rg, d)),
              p['w0'], p['a0'], p['k_k'].reshape(1, d), p['k_a'].reshape(1, d), _head_ones(d)]
    row = pl.BlockSpec((1, tm, d), lambda bi, j: (bi, j, 0))
    row2 = pl.BlockSpec((2, 1, tm, d), lambda bi, j: (0, bi, j, 0))
    halo_prev = pl.BlockSpec((1, SUBLANES, d), lambda bi, j: (bi, jnp.maximum(j * tb - 1, 0), 0))
    halo_next = pl.BlockSpec((1, SUBLANES, d), lambda bi, j: (bi, jnp.minimum((j + 1) * tb, nblk - 1), 0))
    in_specs = [row, halo_prev, halo_next, pl.BlockSpec((1, 2, 6, d), lambda bi, j: (bi, 0, 0, 0))]
    in_specs += [_const_spec(t.shape) for t in consts]
    args = [h, h, h, mod] + consts
    has_vres = v_first is not None
    if has_vres:
        rv = lane_pad(p['v1'].shape[1])
        extra = [p['v0'].reshape(1, d), bf(_pad_to(p['v1'], d, rv)), bf(_pad_to(p['v2'], rv, d))]
        in_specs += [row] + [_const_spec(t.shape) for t in extra]
        args += [v_first] + extra
    one = jax.ShapeDtypeStruct((b, l, d), F32)
    two = jax.ShapeDtypeStruct((2, b, l, d), F32)
    return pl.pallas_call(
        functools.partial(_rw_feat_kernel, lc=lc, l=l, tm=tm, has_vres=has_vres),
        grid=(b, l // tm),
        in_specs=in_specs,
        out_specs=[row2, row2, row2, row, row, row, row],
        out_shape=[two, two, two, one, one, one, one],
        compiler_params=_params("parallel", "parallel"),
        name="rw_features",
    )(*args)


def _tile_rows(x, mask):
    return jnp.concatenate([x.astype(BF16)] * SCAN_G, axis=0) * mask


def _scan_kernel(lwf_ref, kdf_ref, bdf_ref, rf_ref, vf_ref, kkf_ref, lwr_ref, kdr_ref, bdr_ref, rr_ref, vr_ref,
                 kkr_ref, cum_ref, strict_ref, incl_ref, lvl_ref, blk_ref, yf_ref, yr_ref, state_ref):
    c = SCAN_C
    @pl.when(pl.program_id(1) == 0)
    def _():
        state_ref[...] = jnp.zeros_like(state_ref)

    blk = blk_ref[...]
    dir_refs = ((lwf_ref, kdf_ref, bdf_ref, rf_ref, vf_ref, kkf_ref, yf_ref),
                (lwr_ref, kdr_ref, bdr_ref, rr_ref, vr_ref, kkr_ref, yr_ref))
    ngroups = lwf_ref.shape[-1] // SCAN_W
    chains = [(di, g) for di in range(2) for g in range(ngroups)]
    ls = lambda g: slice(g * SCAN_W, (g + 1) * SCAN_W)
    bf = lambda t: t.astype(BF16)
    gc = SCAN_G * c

    lw = [dir_refs[di][0][0, 0, :, ls(g)] for di, g in chains]
    lp = []
    for n, (di, g) in enumerate(chains):
        hi = bf(lw[n])
        lo = bf(lw[n] - hi.astype(F32))
        lp.append(_bdot(cum_ref[di], jnp.concatenate([hi, lo], axis=0)))

    total, ar, bk, bk_tail = [], [], [], []
    for n, (di, g) in enumerate(chains):
        _, kd_ref, bd_ref, r_ref, _, kk_ref, _ = dir_refs[di]
        tot = jnp.sum(lw[n], axis=0, keepdims=True)
        inv = jnp.exp(-lp[n])
        tail = jnp.exp(tot - lp[n])
        kd = kd_ref[0, 0, :, ls(g)]
        bd = bd_ref[0, 0, :, ls(g)]
        at = -kk_ref[0, :, ls(g)] * jnp.exp(lp[n] - lw[n])
        rt = r_ref[0, :, ls(g)] * jnp.exp(lp[n])
        total.append(tot)
        ar.append(bf(jnp.concatenate([at, rt], axis=0)))
        bk.append(jnp.concatenate([_tile_rows(bd * inv, blk), _tile_rows(kd * inv, blk)], axis=0))
        bk_tail.append(bf(jnp.concatenate([bd * tail, kd * tail], axis=0)))

    gg = [_bdot(ar[n], bk[n], NT_DIMS) for n in range(len(chains))]
    sh = [_bdot(ar[n], bf(state_ref[di, g]), NT_DIMS) for n, (di, g) in enumerate(chains)]

    vext = [_tile_rows(dir_refs[di][4][0, :, ls(g)], blk) for di, g in chains]
    rhs = [sh[n][:c] + _bdot(bf(gg[n][:c, gc:] * strict_ref[di]), vext[n]) for n, (di, g) in enumerate(chains)]

    m = [gg[n][:c, :gc] * strict_ref[di] for n, (di, g) in enumerate(chains)]
    for lv in range(lvl_ref.shape[1]):
        m = [m[n] + _bdot(bf(m[n] * lvl_ref[di, lv]), _tile_rows(m[n], blk)) for n, (di, g) in enumerate(chains)]
    u = [rhs[n] + _bdot(bf(m[n]), _tile_rows(rhs[n], blk)) for n in range(len(chains))]

    for n, (di, g) in enumerate(chains):
        incl = incl_ref[di]
        rbk = bf(gg[n][c:] * jnp.concatenate([incl, incl], axis=1))
        y = sh[n][c:] + _bdot(rbk, jnp.concatenate([_tile_rows(u[n], blk), vext[n]], axis=0))
        dir_refs[di][6][0, :, ls(g)] = y
    for n, (di, g) in enumerate(chains):
        uv = bf(jnp.concatenate([u[n], dir_refs[di][4][0, :, ls(g)]], axis=0))
        upd = _bdot(uv, bk_tail[n], TN_DIMS)
        state_ref[di, g] = (state_ref[di, g] * jnp.exp(total[n]) + upd) * blk.astype(F32)


def _scan_masks():
    c, g = SCAN_C, SCAN_G
    t = jnp.arange(c)[:, None]
    s = jnp.arange(c)[None, :]
    cum = jnp.stack([s <= t, s >= t])
    cum = jnp.concatenate([cum, cum], axis=2).astype(BF16)
    sg = jnp.arange(g * c)[None, :] % c
    strict = jnp.stack([sg < t, sg > t]).astype(F32)
    incl = jnp.stack([sg <= t, sg >= t]).astype(F32)
    sizes = [1 << i for i in range(c.bit_length() - 1)]
    lvl = jnp.stack([((t // (2 * s) == sg // (2 * s)) & (t // s != sg // s)).astype(F32) for s in sizes])
    lvl = strict[:, None] * lvl[None]
    rb = jnp.arange(SCAN_W)[:, None] // RW_HEAD
    cb = jnp.arange(SCAN_W)[None, :] // RW_HEAD
    blk = (rb == cb).astype(BF16)
    return cum, strict, incl, lvl, blk


def wkv_scan(lw, kd, bd, r, v, kk, lc):
    _, b, l, d = lw.shape
    c = SCAN_C
    nc = l // c
    ncc = lc // c
    masks = _scan_masks()

    def rev(ci):
        return jnp.where(ci < ncc, ncc - 1 - ci, nc - 1 + ncc - ci)

    fwd2 = pl.BlockSpec((1, 1, c, d), lambda bi, ci: (0, bi, ci, 0))
    rev2 = pl.BlockSpec((1, 1, c, d), lambda bi, ci: (1, bi, rev(ci), 0))
    fwd1 = pl.BlockSpec((1, c, d), lambda bi, ci: (bi, ci, 0))
    rev1 = pl.BlockSpec((1, c, d), lambda bi, ci: (bi, rev(ci), 0))
    out = jax.ShapeDtypeStruct((b, l, d), F32)
    return pl.pallas_call(
        _scan_kernel,
        grid=(b, nc),
        in_specs=[fwd2, fwd2, fwd2, fwd1, fwd1, fwd1, rev2, rev2, rev2, rev1, rev1, rev1]
                 + [_const_spec(t.shape) for t in masks],
        out_specs=[fwd1, rev1],
        out_shape=[out, out],
        scratch_shapes=[pltpu.VMEM((2, d // SCAN_W, SCAN_W, SCAN_W), F32)],
        compiler_params=_params("parallel", "arbitrary"),
        name="wkv_scan",
    )(lw, kd, bd, r, v, kk, lw, kd, bd, r, v, kk, *masks)


def _head_sums(x, gather_ref, spread_ref, exact):
    dot = _dot2 if exact else _bdot
    return _dot2(dot(x, gather_ref[...]), spread_ref[...])


def _rw_readout_kernel(yf_ref, yr_ref, r_ref, v_ref, g_ref, kd_ref, rk_ref, lng_ref, lnb_ref, gather_ref, spread_ref,
                       o_ref):
    inv_n = 1.0 / RW_HEAD
    y = yf_ref[0] + yr_ref[0]
    mu = _head_sums(y, gather_ref, spread_ref, True) * inv_n
    dy = y - mu
    var = _head_sums(dy * dy, gather_ref, spread_ref, False) * inv_n
    yn = dy * lax.rsqrt(var + RW_GN_EPS) * lng_ref[...] + lnb_ref[...]
    kmix = kd_ref[0, 0] * rk_ref[0:1, :] + kd_ref[1, 0] * rk_ref[1:2, :]
    coef = _head_sums(r_ref[0] * kmix, gather_ref, spread_ref, True)
    o_ref[0] = ((yn + coef * v_ref[0]) * g_ref[0]).astype(BF16)


def rw_readout(yf, yr, r, v, g, kd, p):
    b, l, d = r.shape
    tm = _pick(l, (384, 256, 128, 64, 32, 16, 8))
    row = pl.BlockSpec((1, tm, d), lambda bi, j: (bi, j, 0))
    row2 = pl.BlockSpec((2, 1, tm, d), lambda bi, j: (0, bi, j, 0))
    gather = (jnp.arange(d)[:, None] // RW_HEAD == jnp.arange(LANES)[None, :]).astype(BF16)
    consts = [p['r_k'].reshape(2, d), p['ln_g'].reshape(1, d), p['ln_b'].reshape(1, d), gather, gather.T]
    return pl.pallas_call(
        _rw_readout_kernel,
        grid=(b, l // tm),
        in_specs=[row, row, row, row, row, row2] + [_const_spec(t.shape) for t in consts],
        out_specs=row,
        out_shape=jax.ShapeDtypeStruct((b, l, d), BF16),
        compiler_params=_params("parallel", "parallel"),
        name="rw_readout",
    )(yf, yr, r, v, g, kd, *consts)


def kernel(x, c, ctx, c_ctx, ada_w, ada_b, norm_mix, norm_ffn, ffn_w1, ffn_w3, ffn_w2, even_w_in, mla_q_norm, mla_wq_b, mla_kv_norm, mla_wkv_b, mla_q_qk, mla_k_qk, conv_w, conv_b, conv_ln_g, conv_ln_b, even_w_out, rw_mu, rw_wr, rw_wk, rw_wv, rw_w0, rw_w1, rw_w2, rw_a0, rw_a1, rw_a2, rw_v0, rw_v1, rw_v2, rw_k_k, rw_k_a, rw_r_k, rw_g1, rw_g2, rw_ln_g, rw_ln_b, rw_wo):
    b, ll, d = x.shape
    lc = ctx.shape[1]
    depth = ada_w.shape[0]
    bf = lambda t: t.astype(BF16)
    h = jnp.concatenate([ctx, x], axis=1)
    tables = _rope_slot_tables(lc, ll)

    cond = jnp.concatenate([c, c_ctx[None, :]], axis=0)
    cond = jnp.pad(cond, ((0, -(-(b + 1) // SUBLANES) * SUBLANES - (b + 1)), (0, 0)))
    v_first = None
    for i in range(depth):
        j = i // 2
        m = mm(cond, ada_w[i], pre="silu")[:b + 1] + ada_b[i]
        m_lat = m[:b].reshape(b, 1, 6, d)
        m_ctx = jnp.broadcast_to(m[b].reshape(1, 1, 6, d), (b, 1, 6, d))
        mod = jnp.concatenate([m_ctx, m_lat], axis=1)
        if i % 2 == 0:
            p = {'w_in': even_w_in[j], 'q_norm': mla_q_norm[j], 'wq_b': mla_wq_b[j],
                 'kv_norm': mla_kv_norm[j], 'wkv_b': mla_wkv_b[j], 'q_qk': mla_q_qk[j], 'k_qk': mla_k_qk[j]}
            q, k, v, zconv = mla_prep(h, mod, norm_mix[i], p, tables, lc)
            o = attention(q, k, v, lc)
            cv = conformer_conv(zconv, conv_w[j], conv_b[j], conv_ln_g[j], conv_ln_b[j], ((0, lc), (lc, ll)))
            n_attn = o.shape[-1]
            xs = [o, cv]
            wps = [bf(even_w_out[j][:n_attn]), bf(even_w_out[j][n_attn:])]
        else:
            p = {'mu': rw_mu[j], 'wr': rw_wr[j], 'wk': rw_wk[j], 'wv': rw_wv[j],
                 'w0': rw_w0[j], 'w1': rw_w1[j], 'w2': rw_w2[j],
                 'a0': rw_a0[j], 'a1': rw_a1[j], 'a2': rw_a2[j],
                 'k_k': rw_k_k[j], 'k_a': rw_k_a[j], 'r_k': rw_r_k[j],
                 'g1': rw_g1[j], 'g2': rw_g2[j], 'ln_g': rw_ln_g[j], 'ln_b': rw_ln_b[j]}
            if j > 0:
                p['v0'] = rw_v0[j - 1]
                p['v1'] = rw_v1[j - 1]
                p['v2'] = rw_v2[j - 1]
            lw, kd, bd, r, v, kk, g = rw_features(h, mod, norm_mix[i], p, v_first if j > 0 else None, lc)
            if v_first is None:
                v_first = v
            yf, yr = wkv_scan(lw, kd, bd, r, v, kk, lc)
            xs = [rw_readout(yf, yr, r, v, g, kd, p)]
            wps = [bf(rw_wo[j])]
        h = proj_ffn(h, xs, wps, mod, norm_ffn[i], bf(ffn_w1[i]), bf(ffn_w3[i]), bf(ffn_w2[i]), lc)
    return h[:, lc:]
```

```python
import functools
import math

import jax
import jax.numpy as jnp
from jax import lax
from jax.experimental import pallas as pl
from jax.experimental.pallas import tpu as pltpu

F32 = jnp.float32
BF16 = jnp.bfloat16

EPS = 1e-6
GRID_W = 64
ROPE_THETA = 10000.0
MLA_HEADS = 8
MLA_NOPE = 64
MLA_ROPE = 32
MLA_QK = MLA_NOPE + MLA_ROPE
MLA_V = 64
AXIS_DIM = MLA_ROPE // 2
CONV_WIDTH = 31
CONV_LN_EPS = 1e-5
RW_HEAD = 64
RW_GN_EPS = 6.4e-4
DECAY_SCALE = -math.exp(-0.5)

LANES = 128
SUBLANES = 8
VMEM_LIMIT = 56 * 1024 * 1024

SCAN_C = 64
SCAN_G = 4
SCAN_W = SCAN_G * RW_HEAD

NT_DIMS = (((1,), (1,)), ((), ()))
TN_DIMS = (((0,), (0,)), ((), ()))


def _pick(n, cands):
    for c in cands:
        if c <= n and n % c == 0:
            return c
    return n


def _const_spec(shape):
    nd = len(shape)
    return pl.BlockSpec(shape, lambda *_: (0,) * nd, pipeline_mode=pl.Buffered(1))


def _params(*sem):
    return pltpu.CompilerParams(dimension_semantics=sem, vmem_limit_bytes=VMEM_LIMIT)


def _bdot(a, b, dims=None):
    a = a.astype(BF16)
    b = b.astype(BF16)
    if dims is None:
        return jnp.dot(a, b, preferred_element_type=F32)
    return lax.dot_general(a, b, dims, preferred_element_type=F32)


def _dot2(x, w):
    hi = x.astype(BF16)
    lo = (x - hi.astype(F32)).astype(BF16)
    return _bdot(hi, w) + _bdot(lo, w)


def _row_select(mod_ref, first_row, n, lc):
    rows = first_row + lax.broadcasted_iota(jnp.int32, (n, 1), 0)
    is_ctx = rows < lc
    return lambda i: jnp.where(is_ctx, mod_ref[0, 0, i:i + 1, :], mod_ref[0, 1, i:i + 1, :])


def _norm_mod(h, gain, scale, shift):
    return h * lax.rsqrt(jnp.mean(h * h, axis=-1, keepdims=True) + EPS) * gain * (1 + scale) + shift


def _sigmoid(x):
    return 0.5 * jnp.tanh(0.5 * x) + 0.5


def _mm_kernel(x_ref, w_ref, o_ref, *, pre):
    x = x_ref[...]
    if pre == "silu":
        x = x * _sigmoid(x)
    o_ref[...] = jnp.dot(x.astype(BF16), w_ref[...].astype(BF16), preferred_element_type=F32)


def mm(x, w, pre=None):
    m, k = x.shape
    n = w.shape[1]
    tm = _pick(m, (512, 384, 256, 128, 64, 32, 16, 8))
    tn = n if n <= 1024 else _pick(n, (1024, 896, 768, 640, 512, 384, 256, 128))
    return pl.pallas_call(
        functools.partial(_mm_kernel, pre=pre),
        grid=(m // tm, n // tn),
        in_specs=[pl.BlockSpec((tm, k), lambda i, j: (i, 0)),
                  pl.BlockSpec((k, tn), lambda i, j: (0, j))],
        out_specs=pl.BlockSpec((tm, tn), lambda i, j: (i, j)),
        out_shape=jax.ShapeDtypeStruct((m, n), F32),
        compiler_params=_params("parallel", "parallel"),
        name="mm",
    )(x, w)


def _proj_ffn_kernel(*refs, nx, lc, tm, hid_chunk):
    h_ref, mod_ref, gn_ref = refs[0], refs[1], refs[2]
    x_refs = refs[3:3 + nx]
    wp_refs = refs[3 + nx:3 + 2 * nx]
    w1_ref, w3_ref, w2_ref, o_ref = refs[3 + 2 * nx:]
    sel = _row_select(mod_ref, pl.program_id(1) * tm, tm, lc)
    proj = _bdot(x_refs[0][0], wp_refs[0][...])
    for i in range(1, nx):
        proj = proj + _bdot(x_refs[i][0], wp_refs[i][...])
    h1 = h_ref[0] + sel(2) * proj
    f = _norm_mod(h1, gn_ref[...], sel(4), sel(3)).astype(BF16)
    acc = None
    for c0 in range(0, w1_ref.shape[1], hid_chunk):
        a = _bdot(f, w1_ref[:, c0:c0 + hid_chunk])
        b = _bdot(f, w3_ref[:, c0:c0 + hid_chunk])
        part = _bdot(a * _sigmoid(a) * b, w2_ref[c0:c0 + hid_chunk, :])
        acc = part if acc is None else acc + part
    o_ref[0] = h1 + sel(5) * acc


def proj_ffn(h, xs, wps, mod, gn, w1, w3, w2, lc):
    b, l, d = h.shape
    tm = _pick(l, (768, 384, 256, 128, 64, 32, 16, 8))
    nh = w1.shape[1]
    hid_chunk = nh // 2 if (nh // 2) % LANES == 0 else nh
    row = lambda t: pl.BlockSpec((1, tm, t.shape[-1]), lambda bi, j: (bi, j, 0))
    return pl.pallas_call(
        functools.partial(_proj_ffn_kernel, nx=len(xs), lc=lc, tm=tm, hid_chunk=hid_chunk),
        grid=(b, l // tm),
        in_specs=[row(h), pl.BlockSpec((1, 2, 6, d), lambda bi, j: (bi, 0, 0, 0)), _const_spec((1, d))]
                 + [row(x) for x in xs] + [_const_spec(w.shape) for w in wps]
                 + [_const_spec(w1.shape), _const_spec(w3.shape), _const_spec(w2.shape)],
        out_specs=row(h),
        out_shape=jax.ShapeDtypeStruct(h.shape, F32),
        compiler_params=_params("parallel", "parallel"),
        name="proj_ffn",
    )(h, mod, gn.reshape(1, d), *xs, *wps, w1, w3, w2)


def _mla_prep_kernel(h_ref, mod_ref, gn_ref, win_ref, qn_ref, kvn_ref, wq_ref, wk_ref, wv_ref,
                     qg_ref, kg_ref, ones_ref, cos_ref, sin_ref,
                     q_ref, k_ref, v_ref, zc_ref, *, lc, tm, q_rank, kv_rank, scale):
    sel = _row_select(mod_ref, pl.program_id(1) * tm, tm, lc)
    a = _norm_mod(h_ref[0], gn_ref[...], sel(1), sel(0))
    z = _bdot(a, win_ref[...])
    rms = lambda t, g: t * lax.rsqrt(jnp.mean(t * t, axis=-1, keepdims=True) + EPS) * g
    zq = rms(z[:, :q_rank], qn_ref[...]).astype(BF16)
    zkv = rms(z[:, q_rank:q_rank + kv_rank], kvn_ref[...]).astype(BF16)
    slots = wq_ref.shape[1]
    k0 = q_rank + kv_rank
    q = _bdot(zq, wq_ref[...])
    k = _bdot(zkv, wk_ref[...]) + z[:, k0:k0 + slots]
    v_ref[0] = _bdot(zkv, wv_ref[...]).astype(BF16)
    zc_ref[0] = z[:, k0 + slots:]

    lane = lax.broadcasted_iota(jnp.int32, (1, slots), 1) % LANES
    first_half = ((lane - MLA_NOPE) % AXIS_DIM) < AXIS_DIM // 2
    cos = cos_ref[...]
    sin = sin_ref[...]

    def head_norm_rope(t, gain):
        ss = _bdot(t * t, ones_ref[...]) * (1.0 / MLA_QK)
        t = t * lax.rsqrt(ss + EPS) * gain
        partner = jnp.where(first_half, pltpu.roll(t, slots - AXIS_DIM // 2, axis=1),
                            pltpu.roll(t, AXIS_DIM // 2, axis=1))
        return t * cos + partner * sin

    q_ref[0] = (head_norm_rope(q, qg_ref[...]) * scale).astype(BF16)
    k_ref[0] = head_norm_rope(k, kg_ref[...]).astype(BF16)


def _head_slots(w, width):
    kdim = w.shape[0]
    w = w.reshape(kdim, MLA_HEADS, width)
    return jnp.pad(w, ((0, 0), (0, 0), (0, LANES - width))).reshape(kdim, MLA_HEADS * LANES)


def _rope_slot_tables(lc, ll):
    rows = ll // GRID_W
    row = jnp.repeat(jnp.arange(rows, dtype=F32), GRID_W)
    col = jnp.tile(jnp.arange(GRID_W, dtype=F32), rows)
    inv = ROPE_THETA ** (-jnp.arange(0, AXIS_DIM, 2, dtype=F32) / AXIS_DIM)
    cos_parts, sin_parts = [], []
    for pos in (row, col):
        ang = pos[:, None] * inv
        cos_parts += [jnp.cos(ang), jnp.cos(ang)]
        sin_parts += [-jnp.sin(ang), jnp.sin(ang)]
    ones = jnp.ones((ll, MLA_NOPE), F32)
    zeros = jnp.zeros((ll, MLA_NOPE), F32)
    tail1 = jnp.ones((ll, LANES - MLA_QK), F32)
    tail0 = jnp.zeros((ll, LANES - MLA_QK), F32)
    cos = jnp.concatenate([ones] + cos_parts + [tail1], axis=1)
    sin = jnp.concatenate([zeros] + sin_parts + [tail0], axis=1)
    cos = jnp.concatenate([jnp.ones((lc, LANES), F32), cos], axis=0)
    sin = jnp.concatenate([jnp.zeros((lc, LANES), F32), sin], axis=0)
    return jnp.tile(cos, (1, MLA_HEADS)), jnp.tile(sin, (1, MLA_HEADS))


def mla_prep(h, mod, gn, p, tables, lc):
    b, l, d = h.shape
    q_rank = p['q_norm'].shape[0]
    kv_rank = p['kv_norm'].shape[0]
    slots = MLA_HEADS * LANES
    w_in = p['w_in']
    k0 = q_rank + kv_rank
    w_rope = jnp.tile(jnp.pad(w_in[:, k0:k0 + MLA_ROPE], ((0, 0), (MLA_NOPE, LANES - MLA_QK))), (1, MLA_HEADS))
    w_ext = jnp.concatenate([w_in[:, :k0], w_rope, w_in[:, k0 + MLA_ROPE:]], axis=1).astype(BF16)
    n_conv = w_in.shape[1] - k0 - MLA_ROPE
    wkv = p['wkv_b'].reshape(kv_rank, MLA_HEADS, MLA_NOPE + MLA_V)
    wq = _head_slots(p['wq_b'], MLA_QK).astype(BF16)
    wk = _head_slots(wkv[:, :, :MLA_NOPE].reshape(kv_rank, -1), MLA_NOPE).astype(BF16)
    wv = wkv[:, :, MLA_NOPE:].reshape(kv_rank, MLA_HEADS * MLA_V).astype(BF16)
    slot_gain = lambda g: jnp.tile(jnp.pad(g, (0, LANES - MLA_QK)), MLA_HEADS).reshape(1, slots)
    sid = jnp.arange(slots) // LANES
    ones = (sid[:, None] == sid[None, :]).astype(BF16)
    tm = _pick(l, (384, 256, 128, 64, 32, 16, 8))
    cos, sin = tables
    row = lambda n: pl.BlockSpec((1, tm, n), lambda bi, j: (bi, j, 0))
    tab = pl.BlockSpec((tm, slots), lambda bi, j: (j, 0))
    consts = [gn.reshape(1, d), w_ext, p['q_norm'].reshape(1, -1), p['kv_norm'].reshape(1, -1), wq, wk, wv,
              slot_gain(p['q_qk']), slot_gain(p['k_qk']), ones]
    return pl.pallas_call(
        functools.partial(_mla_prep_kernel, lc=lc, tm=tm, q_rank=q_rank, kv_rank=kv_rank, scale=MLA_QK ** -0.5),
        grid=(b, l // tm),
        in_specs=[row(d), pl.BlockSpec((1, 2, 6, d), lambda bi, j: (bi, 0, 0, 0))]
                 + [_const_spec(t.shape) for t in consts] + [tab, tab],
        out_specs=[row(slots), row(slots), row(MLA_HEADS * MLA_V), row(n_conv)],
        out_shape=[jax.ShapeDtypeStruct((b, l, slots), BF16), jax.ShapeDtypeStruct((b, l, slots), BF16),
                   jax.ShapeDtypeStruct((b, l, MLA_HEADS * MLA_V), BF16),
                   jax.ShapeDtypeStruct((b, l, n_conv), F32)],
        compiler_params=_params("parallel", "parallel"),
        name="mla_prep",
    )(h, mod, *consts, cos, sin)


def _attn_kernel(q_ref, k_ref, v_ref, o_ref):
    lane = lax.broadcasted_iota(jnp.int32, (1, LANES), 1)
    for hp in range(MLA_HEADS // 2):
        vp = v_ref[0, :, hp * LANES:(hp + 1) * LANES]
        outs = []
        for hh in range(2):
            hs = slice((2 * hp + hh) * LANES, (2 * hp + hh + 1) * LANES)
            s = lax.dot_general(q_ref[0, :, hs], k_ref[0, :, hs], NT_DIMS, preferred_element_type=F32)
            p = jnp.exp(s - jnp.max(s, axis=-1, keepdims=True))
            l = jnp.sum(p, axis=-1, keepdims=True)
            outs.append(jnp.dot(p.astype(BF16), vp, preferred_element_type=F32) / l)
        o_ref[0, :, hp * LANES:(hp + 1) * LANES] = jnp.where(lane < MLA_V, outs[0], outs[1]).astype(BF16)


def attention(q, k, v, klen):
    b, lq, slots = q.shape
    dv = v.shape[-1]
    tq = _pick(lq, (512, 256, 128, 64, 32, 16, 8))
    return pl.pallas_call(
        _attn_kernel,
        grid=(b, lq // tq),
        in_specs=[pl.BlockSpec((1, tq, slots), lambda bi, i: (bi, i, 0)),
                  pl.BlockSpec((1, klen, slots), lambda bi, i: (bi, 0, 0)),
                  pl.BlockSpec((1, klen, dv), lambda bi, i: (bi, 0, 0))],
        out_specs=pl.BlockSpec((1, tq, dv), lambda bi, i: (bi, i, 0)),
        out_shape=jax.ShapeDtypeStruct((b, lq, dv), BF16),
        compiler_params=_params("parallel", "parallel"),
        name="attention",
    )(q, k, v)


CONV_ROWS = 64
CONV_HALO = 16


def _conv_kernel(u_ref, w_ref, b_ref, g_ref, beta_ref, o_ref, pad_ref, y_ref, *, segments, ch):
    half = CONV_WIDTH // 2
    win = CONV_ROWS + 2 * CONV_HALO
    nlb = ch // LANES
    for seg_start, seg_len in segments:
        zeros = jnp.zeros((CONV_HALO, ch), F32)
        pad_ref[0:CONV_HALO, :] = zeros
        pad_ref[CONV_HALO + seg_len:2 * CONV_HALO + seg_len, :] = zeros

        def glu_body(i, carry):
            r0 = pl.multiple_of(i * CONV_ROWS, CONV_ROWS)
            u = u_ref[0, pl.ds(seg_start + r0, CONV_ROWS), :]
            pad_ref[pl.ds(CONV_HALO + r0, CONV_ROWS), :] = u[:, :ch] * _sigmoid(u[:, ch:])
            return carry

        lax.fori_loop(0, seg_len // CONV_ROWS, glu_body, 0)

        def conv_body(i, carry):
            r0 = pl.multiple_of(i * CONV_ROWS, CONV_ROWS)
            for lb in range(nlb):
                ls = slice(lb * LANES, (lb + 1) * LANES)
                xwin = pad_ref[pl.ds(r0, win), ls]
                acc = jnp.zeros((CONV_ROWS, LANES), F32)
                for j in range(CONV_WIDTH):
                    off = CONV_HALO - half + j
                    shifted = pltpu.roll(xwin, shift=(win - off) % win, axis=0)[0:CONV_ROWS]
                    acc = acc + shifted * w_ref[j:j + 1, ls]
                y_ref[pl.ds(seg_start + r0, CONV_ROWS), ls] = acc + b_ref[0:1, ls]
            return carry

        lax.fori_loop(0, seg_len // CONV_ROWS, conv_body, 0)

    def ln_body(i, carry):
        r0 = pl.multiple_of(i * CONV_ROWS, CONV_ROWS)
        y = y_ref[pl.ds(r0, CONV_ROWS), :]
        mu = jnp.mean(y, axis=-1, keepdims=True)
        d = y - mu
        var = jnp.mean(d * d, axis=-1, keepdims=True)
        z = d * lax.rsqrt(var + CONV_LN_EPS) * g_ref[0:1, :] + beta_ref[0:1, :]
        o_ref[0, pl.ds(r0, CONV_ROWS), :] = (z * _sigmoid(z)).astype(o_ref.dtype)
        return carry

    total = sum(s[1] for s in segments)
    lax.fori_loop(0, total // CONV_ROWS, ln_body, 0)


def conformer_conv(u, conv_w, conv_b, ln_g, ln_b, segments):
    b, l, ch2 = u.shape
    ch = ch2 // 2
    max_seg = max(s[1] for s in segments)
    row = lambda t: t.reshape(1, ch)
    return pl.pallas_call(
        functools.partial(_conv_kernel, segments=segments, ch=ch),
        grid=(b,),
        in_specs=[pl.BlockSpec((1, l, ch2), lambda i: (i, 0, 0)),
                  pl.BlockSpec((CONV_WIDTH, ch), lambda i: (0, 0)),
                  pl.BlockSpec((1, ch), lambda i: (0, 0)),
                  pl.BlockSpec((1, ch), lambda i: (0, 0)),
                  pl.BlockSpec((1, ch), lambda i: (0, 0))],
        out_specs=pl.BlockSpec((1, l, ch), lambda i: (i, 0, 0)),
        out_shape=jax.ShapeDtypeStruct((b, l, ch), BF16),
        scratch_shapes=[pltpu.VMEM((max_seg + 2 * CONV_HALO, ch), F32),
                        pltpu.VMEM((l, ch), F32)],
        compiler_params=_params("parallel"),
        name="conformer_conv",
    )(u, conv_w, row(conv_b), row(ln_g), row(ln_b))


def _rw_feat_kernel(*refs, lc, l, tm, has_vres):
    (h_ref, hp_ref, hn_ref, mod_ref, gn_ref, mu_ref, wr_ref, wk_ref, wv_ref, w1_ref, w2_ref, a1_ref, a2_ref,
     g1_ref, g2_ref, w0_ref, a0_ref, kk_gain_ref, ka_ref, ones_ref) = refs[:20]
    rest = refs[20:]
    if has_vres:
        vf_ref, v0_ref, v1_ref, v2_ref = rest[:4]
        rest = rest[4:]
    dir_ref, sh_ref, g_ref = rest
    d = h_ref.shape[-1]
    first = pl.program_id(1) * tm
    n_ext = tm + 2 * SUBLANES
    sel = _row_select(mod_ref, first - SUBLANES, n_ext, lc)
    h_ext = jnp.concatenate([hp_ref[0], h_ref[0], hn_ref[0]], axis=0)
    a_ext = _norm_mod(h_ext, gn_ref[...], sel(1), sel(0))
    cur = a_ext[SUBLANES:SUBLANES + tm]
    pos = first + lax.broadcasted_iota(jnp.int32, (tm, 1), 0)
    prev = pltpu.roll(a_ext, 1, axis=0)[SUBLANES:SUBLANES + tm]
    nxt = pltpu.roll(a_ext, n_ext - 1, axis=0)[SUBLANES:SUBLANES + tm]
    d_prev = jnp.where((pos == 0) | (pos == lc), 0.0, prev) - cur
    d_next = jnp.where((pos == lc - 1) | (pos == l - 1), 0.0, nxt) - cur
    mix = lambda i: (cur + d_prev * mu_ref[0, i:i + 1, :] + d_next * mu_ref[1, i:i + 1, :]).astype(BF16)

    xv = mix(3)
    k = _bdot(mix(2), wk_ref[...])
    v = _bdot(xv, wv_ref[...])
    if has_vres:
        gate = _sigmoid(v0_ref[...] + _bdot(_bdot(xv, v1_ref[...]), v2_ref[...]))
        v = v + (vf_ref[0] - v) * gate
    sh_ref[0, :, d:2 * d] = v
    sh_ref[0, :, 0:d] = _bdot(mix(0), wr_ref[...])
    g_ref[0] = _bdot(_sigmoid(_bdot(mix(5), g1_ref[...])), g2_ref[...])

    kkf = k * kk_gain_ref[...]
    ss = _bdot(kkf * kkf, ones_ref[...])
    kk = kkf * lax.rsqrt(jnp.maximum(ss, 1e-24))
    sh_ref[0, :, 2 * d:3 * d] = kk
    w_lora = _bdot(jnp.tanh(_bdot(mix(1), w1_ref[...])), w2_ref[...])
    a_lora = _bdot(_bdot(mix(4), a1_ref[...]), a2_ref[...])
    for di in range(2):
        dir_ref[di, 0, :, 0:d] = DECAY_SCALE * _sigmoid(w0_ref[di:di + 1, :] + w_lora[:, di * d:(di + 1) * d])
        a_rate = _sigmoid(a0_ref[di:di + 1, :] + a_lora[:, di * d:(di + 1) * d])
        dir_ref[di, 0, :, d:2 * d] = k * (1 + (a_rate - 1) * ka_ref[...])
        dir_ref[di, 0, :, 2 * d:3 * d] = kk * a_rate


def _pad_to(w, rows, cols):
    return jnp.pad(w, ((0, rows - w.shape[0]), (0, cols - w.shape[1])))


def _head_ones(d):
    hid = jnp.arange(d) // RW_HEAD
    return (hid[:, None] == hid[None, :]).astype(BF16)


def rw_features(h, mod, gn, p, v_first, lc):
    b, l, d = h.shape
    tm = _pick(lc, (256, 128, 64, 32, 16, 8))
    nblk = l // SUBLANES
    tb = tm // SUBLANES
    bf = lambda t: t.astype(BF16)
    cat2 = lambda t: jnp.concatenate([t[0], t[1]], axis=1)
    bdiag = lambda t: jnp.concatenate(
        [jnp.concatenate([t[0], jnp.zeros_like(t[0])], axis=1),
         jnp.concatenate([jnp.zeros_like(t[1]), t[1]], axis=1)], axis=0)
    lane_pad = lambda n: -(-n // LANES) * LANES
    rg = lane_pad(p['g1'].shape[1])
    consts = [gn.reshape(1, d), p['mu'], bf(p['wr']), bf(p['wk']), bf(p['wv']),
              bf(cat2(p['w1'])), bf(bdiag(p['w2'])), bf(cat2(p['a1'])), bf(bdiag(p['a2'])),
              bf(_pad_to(p['g1'], d, rg)), bf(_pad_to(p['g2'], rg, d)),
              p['w0'], p['a0'], p['k_k'].reshape(1, d), p['k_a'].reshape(1, d), _head_ones(d)]
    row = pl.BlockSpec((1, tm, d), lambda bi, j: (bi, j, 0))
    halo_prev = pl.BlockSpec((1, SUBLANES, d), lambda bi, j: (bi, jnp.maximum(j * tb - 1, 0), 0))
    halo_next = pl.BlockSpec((1, SUBLANES, d), lambda bi, j: (bi, jnp.minimum((j + 1) * tb, nblk - 1), 0))
    in_specs = [row, halo_prev, halo_next, pl.BlockSpec((1, 2, 6, d), lambda bi, j: (bi, 0, 0, 0))]
    in_specs += [_const_spec(t.shape) for t in consts]
    args = [h, h, h, mod] + consts
    has_vres = v_first is not None
    if has_vres:
        rv = lane_pad(p['v1'].shape[1])
        extra = [p['v0'].reshape(1, d), bf(_pad_to(p['v1'], d, rv)), bf(_pad_to(p['v2'], rv, d))]
        in_specs += [pl.BlockSpec((1, tm, d), lambda bi, j: (bi, j, 1))] + [_const_spec(t.shape) for t in extra]
        args += [v_first] + extra
    return pl.pallas_call(
        functools.partial(_rw_feat_kernel, lc=lc, l=l, tm=tm, has_vres=has_vres),
        grid=(b, l // tm),
        in_specs=in_specs,
        out_specs=[pl.BlockSpec((2, 1, tm, 3 * d), lambda bi, j: (0, bi, j, 0)),
                   pl.BlockSpec((1, tm, 3 * d), lambda bi, j: (bi, j, 0)), row],
        out_shape=[jax.ShapeDtypeStruct((2, b, l, 3 * d), F32), jax.ShapeDtypeStruct((b, l, 3 * d), F32),
                   jax.ShapeDtypeStruct((b, l, d), F32)],
        compiler_params=_params("parallel", "parallel"),
        name="rw_features",
    )(*args)


def _tile_rows(x, mask):
    return jnp.concatenate([x.astype(BF16)] * SCAN_G, axis=0) * mask


def _scan_kernel(dirf_ref, shf_ref, dirr_ref, shr_ref, cum_ref, strict_ref, incl_ref, lvl_ref, blk_ref,
                 yf_ref, yr_ref, state_ref):
    c = SCAN_C
    @pl.when(pl.program_id(1) == 0)
    def _():
        state_ref[...] = jnp.zeros_like(state_ref)

    blk = blk_ref[...]
    d = yf_ref.shape[-1]
    dir_refs = (dirf_ref, dirr_ref)
    sh_refs = (shf_ref, shr_ref)
    y_refs = (yf_ref, yr_ref)
    chains = [(di, g) for di in range(2) for g in range(d // SCAN_W)]
    ls = lambda g, part=0: slice(part * d + g * SCAN_W, part * d + (g + 1) * SCAN_W)
    bf = lambda t: t.astype(BF16)
    gc = SCAN_G * c

    lw = [dir_refs[di][0, 0, :, ls(g)] for di, g in chains]
    lp = []
    for n, (di, g) in enumerate(chains):
        hi = bf(lw[n])
        lo = bf(lw[n] - hi.astype(F32))
        lp.append(_bdot(cum_ref[di], jnp.concatenate([hi, lo], axis=0)))

    total, ar, bk, bk_tail = [], [], [], []
    for n, (di, g) in enumerate(chains):
        tot = jnp.sum(lw[n], axis=0, keepdims=True)
        inv = jnp.exp(-lp[n])
        tail = jnp.exp(tot - lp[n])
        kd = dir_refs[di][0, 0, :, ls(g, 1)]
        bd = dir_refs[di][0, 0, :, ls(g, 2)]
        at = -sh_refs[di][0, :, ls(g, 2)] * jnp.exp(lp[n] - lw[n])
        rt = sh_refs[di][0, :, ls(g, 0)] * jnp.exp(lp[n])
        total.append(tot)
        ar.append(bf(jnp.concatenate([at, rt], axis=0)))
        bk.append(jnp.concatenate([_tile_rows(bd * inv, blk), _tile_rows(kd * inv, blk)], axis=0))
        bk_tail.append(bf(jnp.concatenate([bd * tail, kd * tail], axis=0)))

    gg = [_bdot(ar[n], bk[n], NT_DIMS) for n in range(len(chains))]
    sh = [_bdot(ar[n], bf(state_ref[di, g]), NT_DIMS) for n, (di, g) in enumerate(chains)]

    vext = [_tile_rows(sh_refs[di][0, :, ls(g, 1)], blk) for di, g in chains]
    rhs = [sh[n][:c] + _bdot(bf(gg[n][:c, gc:] * strict_ref[di]), vext[n]) for n, (di, g) in enumerate(chains)]

    m = [gg[n][:c, :gc] * strict_ref[di] for n, (di, g) in enumerate(chains)]
    for lv in range(lvl_ref.shape[1]):
        m = [m[n] + _bdot(bf(m[n] * lvl_ref[di, lv]), _tile_rows(m[n], blk)) for n, (di, g) in enumerate(chains)]
    u = [rhs[n] + _bdot(bf(m[n]), _tile_rows(rhs[n], blk)) for n in range(len(chains))]

    for n, (di, g) in enumerate(chains):
        incl = incl_ref[di]
        rbk = bf(gg[n][c:] * jnp.concatenate([incl, incl], axis=1))
        y = sh[n][c:] + _bdot(rbk, jnp.concatenate([_tile_rows(u[n], blk), vext[n]], axis=0))
        y_refs[di][0, :, ls(g)] = y
    for n, (di, g) in enumerate(chains):
        uv = bf(jnp.concatenate([u[n], sh_refs[di][0, :, ls(g, 1)]], axis=0))
        upd = _bdot(uv, bk_tail[n], TN_DIMS)
        state_ref[di, g] = (state_ref[di, g] * jnp.exp(total[n]) + upd) * blk.astype(F32)


def _scan_masks():
    c, g = SCAN_C, SCAN_G
    t = jnp.arange(c)[:, None]
    s = jnp.arange(c)[None, :]
    cum = jnp.stack([s <= t, s >= t])
    cum = jnp.concatenate([cum, cum], axis=2).astype(BF16)
    sg = jnp.arange(g * c)[None, :] % c
    strict = jnp.stack([sg < t, sg > t]).astype(F32)
    incl = jnp.stack([sg <= t, sg >= t]).astype(F32)
    sizes = [1 << i for i in range(c.bit_length() - 1)]
    lvl = jnp.stack([((t // (2 * s) == sg // (2 * s)) & (t // s != sg // s)).astype(F32) for s in sizes])
    lvl = strict[:, None] * lvl[None]
    rb = jnp.arange(SCAN_W)[:, None] // RW_HEAD
    cb = jnp.arange(SCAN_W)[None, :] // RW_HEAD
    blk = (rb == cb).astype(BF16)
    return cum, strict, incl, lvl, blk


def wkv_scan(dirs, shared, lc):
    b, l, d3 = shared.shape
    d = d3 // 3
    c = SCAN_C
    nc = l // c
    ncc = lc // c
    masks = _scan_masks()

    def rev(ci):
        return jnp.where(ci < ncc, ncc - 1 - ci, nc - 1 + ncc - ci)

    fwd2 = pl.BlockSpec((1, 1, c, d3), lambda bi, ci: (0, bi, ci, 0))
    rev2 = pl.BlockSpec((1, 1, c, d3), lambda bi, ci: (1, bi, rev(ci), 0))
    fwd1 = pl.BlockSpec((1, c, d3), lambda bi, ci: (bi, ci, 0))
    rev1 = pl.BlockSpec((1, c, d3), lambda bi, ci: (bi, rev(ci), 0))
    out = jax.ShapeDtypeStruct((b, l, d), F32)
    return pl.pallas_call(
        _scan_kernel,
        grid=(b, nc),
        in_specs=[fwd2, fwd1, rev2, rev1] + [_const_spec(t.shape) for t in masks],
        out_specs=[pl.BlockSpec((1, c, d), lambda bi, ci: (bi, ci, 0)),
                   pl.BlockSpec((1, c, d), lambda bi, ci: (bi, rev(ci), 0))],
        out_shape=[out, out],
        scratch_shapes=[pltpu.VMEM((2, d // SCAN_W, SCAN_W, SCAN_W), F32)],
        compiler_params=_params("parallel", "arbitrary"),
        name="wkv_scan",
    )(dirs, shared, dirs, shared, *masks)


def _head_sums(x, gather_ref, spread_ref, exact):
    dot = _dot2 if exact else _bdot
    return _dot2(dot(x, gather_ref[...]), spread_ref[...])


def _rw_readout_kernel(yf_ref, yr_ref, r_ref, v_ref, g_ref, kd_ref, rk_ref, lng_ref, lnb_ref, gather_ref, spread_ref,
                       o_ref):
    inv_n = 1.0 / RW_HEAD
    y = yf_ref[0] + yr_ref[0]
    mu = _head_sums(y, gather_ref, spread_ref, True) * inv_n
    dy = y - mu
    var = _head_sums(dy * dy, gather_ref, spread_ref, False) * inv_n
    yn = dy * lax.rsqrt(var + RW_GN_EPS) * lng_ref[...] + lnb_ref[...]
    kmix = kd_ref[0, 0] * rk_ref[0:1, :] + kd_ref[1, 0] * rk_ref[1:2, :]
    coef = _head_sums(r_ref[0] * kmix, gather_ref, spread_ref, True)
    o_ref[0] = ((yn + coef * v_ref[0]) * g_ref[0]).astype(BF16)


def rw_readout(yf, yr, shared, g, dirs, p):
    b, l, d = g.shape
    tm = _pick(l, (384, 256, 128, 64, 32, 16, 8))
    row = pl.BlockSpec((1, tm, d), lambda bi, j: (bi, j, 0))
    v_row = pl.BlockSpec((1, tm, d), lambda bi, j: (bi, j, 1))
    kd_rows = pl.BlockSpec((2, 1, tm, d), lambda bi, j: (0, bi, j, 1))
    gather = (jnp.arange(d)[:, None] // RW_HEAD == jnp.arange(LANES)[None, :]).astype(BF16)
    consts = [p['r_k'].reshape(2, d), p['ln_g'].reshape(1, d), p['ln_b'].reshape(1, d), gather, gather.T]
    return pl.pallas_call(
        _rw_readout_kernel,
        grid=(b, l // tm),
        in_specs=[row, row, row, v_row, row, kd_rows] + [_const_spec(t.shape) for t in consts],
        out_specs=row,
        out_shape=jax.ShapeDtypeStruct((b, l, d), BF16),
        compiler_params=_params("parallel", "parallel"),
        name="rw_readout",
    )(yf, yr, shared, shared, g, dirs, *consts)


def kernel(x, c, ctx, c_ctx, ada_w, ada_b, norm_mix, norm_ffn, ffn_w1, ffn_w3, ffn_w2, even_w_in, mla_q_norm, mla_wq_b, mla_kv_norm, mla_wkv_b, mla_q_qk, mla_k_qk, conv_w, conv_b, conv_ln_g, conv_ln_b, even_w_out, rw_mu, rw_wr, rw_wk, rw_wv, rw_w0, rw_w1, rw_w2, rw_a0, rw_a1, rw_a2, rw_v0, rw_v1, rw_v2, rw_k_k, rw_k_a, rw_r_k, rw_g1, rw_g2, rw_ln_g, rw_ln_b, rw_wo):
    b, ll, d = x.shape
    lc = ctx.shape[1]
    depth = ada_w.shape[0]
    bf = lambda t: t.astype(BF16)
    h = jnp.concatenate([ctx, x], axis=1)
    tables = _rope_slot_tables(lc, ll)

    cond = jnp.concatenate([c, c_ctx[None, :]], axis=0)
    cond = jnp.pad(cond, ((0, -(-(b + 1) // SUBLANES) * SUBLANES - (b + 1)), (0, 0)))
    v_first = None
    for i in range(depth):
        j = i // 2
        m = mm(cond, ada_w[i], pre="silu")[:b + 1] + ada_b[i]
        m_lat = m[:b].reshape(b, 1, 6, d)
        m_ctx = jnp.broadcast_to(m[b].reshape(1, 1, 6, d), (b, 1, 6, d))
        mod = jnp.concatenate([m_ctx, m_lat], axis=1)
        if i % 2 == 0:
            p = {'w_in': even_w_in[j], 'q_norm': mla_q_norm[j], 'wq_b': mla_wq_b[j],
                 'kv_norm': mla_kv_norm[j], 'wkv_b': mla_wkv_b[j], 'q_qk': mla_q_qk[j], 'k_qk': mla_k_qk[j]}
            q, k, v, zconv = mla_prep(h, mod, norm_mix[i], p, tables, lc)
            o = jnp.concatenate([attention(q[:, :lc], k, v, lc), attention(q[:, lc:], k, v, lc + ll)], axis=1)
            cv = conformer_conv(zconv, conv_w[j], conv_b[j], conv_ln_g[j], conv_ln_b[j], ((0, lc), (lc, ll)))
            n_attn = o.shape[-1]
            xs = [o, cv]
            wps = [bf(even_w_out[j][:n_attn]), bf(even_w_out[j][n_attn:])]
        else:
            p = {'mu': rw_mu[j], 'wr': rw_wr[j], 'wk': rw_wk[j], 'wv': rw_wv[j],
                 'w0': rw_w0[j], 'w1': rw_w1[j], 'w2': rw_w2[j],
                 'a0': rw_a0[j], 'a1': rw_a1[j], 'a2': rw_a2[j],
                 'k_k': rw_k_k[j], 'k_a': rw_k_a[j], 'r_k': rw_r_k[j],
                 'g1': rw_g1[j], 'g2': rw_g2[j], 'ln_g': rw_ln_g[j], 'ln_b': rw_ln_b[j]}
            if j > 0:
                p['v0'] = rw_v0[j - 1]
                p['v1'] = rw_v1[j - 1]
                p['v2'] = rw_v2[j - 1]
            dirs, shared, g = rw_features(h, mod, norm_mix[i], p, v_first if j > 0 else None, lc)
            if v_first is None:
                v_first = shared
            yf, yr = wkv_scan(dirs, shared, lc)
            xs = [rw_readout(yf, yr, shared, g, dirs, p)]
            wps = [bf(rw_wo[j])]
        h = proj_ffn(h, xs, wps, mod, norm_ffn[i], bf(ffn_w1[i]), bf(ffn_w3[i]), bf(ffn_w2[i]), lc)
    return h[:, lc:]
```

```python
import functools
import math

import jax
import jax.numpy as jnp
from jax import lax
from jax.experimental import pallas as pl
from jax.experimental.pallas import tpu as pltpu

F32 = jnp.float32
BF16 = jnp.bfloat16

EPS = 1e-6
GRID_W = 64
ROPE_THETA = 10000.0
MLA_HEADS = 8
MLA_NOPE = 64
MLA_ROPE = 32
MLA_QK = MLA_NOPE + MLA_ROPE
MLA_V = 64
AXIS_DIM = MLA_ROPE // 2
CONV_WIDTH = 31
CONV_LN_EPS = 1e-5
RW_HEAD = 64
RW_GN_EPS = 6.4e-4
DECAY_SCALE = -math.exp(-0.5)

LANES = 128
SUBLANES = 8
VMEM_LIMIT = 56 * 1024 * 1024

SCAN_C = 64
SCAN_SUB = 4
SCAN_G = 4
SCAN_W = SCAN_G * RW_HEAD

NT_DIMS = (((1,), (1,)), ((), ()))
TN_DIMS = (((0,), (0,)), ((), ()))


def _pick(n, cands):
    for c in cands:
        if c <= n and n % c == 0:
            return c
    return n


def _const_spec(shape):
    nd = len(shape)
    return pl.BlockSpec(shape, lambda *_: (0,) * nd, pipeline_mode=pl.Buffered(1))


def _params(*sem):
    return pltpu.CompilerParams(dimension_semantics=sem, vmem_limit_bytes=VMEM_LIMIT)


def _bdot(a, b, dims=None):
    a = a.astype(BF16)
    b = b.astype(BF16)
    if dims is None:
        return jnp.dot(a, b, preferred_element_type=F32)
    return lax.dot_general(a, b, dims, preferred_element_type=F32)


def _dot2(x, w):
    hi = x.astype(BF16)
    lo = (x - hi.astype(F32)).astype(BF16)
    return _bdot(hi, w) + _bdot(lo, w)


def _row_select(mod_ref, first_row, n, lc):
    rows = first_row + lax.broadcasted_iota(jnp.int32, (n, 1), 0)
    is_ctx = rows < lc
    return lambda i: jnp.where(is_ctx, mod_ref[0, 0, i:i + 1, :], mod_ref[0, 1, i:i + 1, :])


def _norm_mod(h, gain, scale, shift):
    return h * lax.rsqrt(jnp.mean(h * h, axis=-1, keepdims=True) + EPS) * gain * (1 + scale) + shift


def _sigmoid(x):
    return 0.5 * jnp.tanh(0.5 * x) + 0.5


def _mm_kernel(x_ref, w_ref, o_ref, *, pre):
    x = x_ref[...]
    if pre == "silu":
        x = x * _sigmoid(x)
    o_ref[...] = jnp.dot(x.astype(BF16), w_ref[...].astype(BF16), preferred_element_type=F32)


def mm(x, w, pre=None):
    m, k = x.shape
    n = w.shape[1]
    tm = _pick(m, (512, 384, 256, 128, 64, 32, 16, 8))
    tn = n if n <= 1024 else _pick(n, (1024, 896, 768, 640, 512, 384, 256, 128))
    return pl.pallas_call(
        functools.partial(_mm_kernel, pre=pre),
        grid=(m // tm, n // tn),
        in_specs=[pl.BlockSpec((tm, k), lambda i, j: (i, 0)),
                  pl.BlockSpec((k, tn), lambda i, j: (0, j))],
        out_specs=pl.BlockSpec((tm, tn), lambda i, j: (i, j)),
        out_shape=jax.ShapeDtypeStruct((m, n), F32),
        compiler_params=_params("parallel", "parallel"),
        name="mm",
    )(x, w)


def _proj_ffn_kernel(*refs, nx, lc, tm, hid_chunk):
    h_ref, mod_ref, gn_ref = refs[0], refs[1], refs[2]
    x_refs = refs[3:3 + nx]
    wp_refs = refs[3 + nx:3 + 2 * nx]
    w1_ref, w3_ref, w2_ref, o_ref = refs[3 + 2 * nx:]
    sel = _row_select(mod_ref, pl.program_id(1) * tm, tm, lc)
    proj = _bdot(x_refs[0][0], wp_refs[0][...])
    for i in range(1, nx):
        proj = proj + _bdot(x_refs[i][0], wp_refs[i][...])
    h1 = h_ref[0] + sel(2) * proj
    f = _norm_mod(h1, gn_ref[...], sel(4), sel(3)).astype(BF16)
    acc = None
    for c0 in range(0, w1_ref.shape[1], hid_chunk):
        a = _bdot(f, w1_ref[:, c0:c0 + hid_chunk])
        b = _bdot(f, w3_ref[:, c0:c0 + hid_chunk])
        part = _bdot(a * _sigmoid(a) * b, w2_ref[c0:c0 + hid_chunk, :])
        acc = part if acc is None else acc + part
    o_ref[0] = h1 + sel(5) * acc


def proj_ffn(h, xs, wps, mod, gn, w1, w3, w2, lc):
    b, l, d = h.shape
    tm = _pick(l, (768, 384, 256, 128, 64, 32, 16, 8))
    nh = w1.shape[1]
    hid_chunk = nh // 2 if (nh // 2) % LANES == 0 else nh
    row = lambda t: pl.BlockSpec((1, tm, t.shape[-1]), lambda bi, j: (bi, j, 0))
    return pl.pallas_call(
        functools.partial(_proj_ffn_kernel, nx=len(xs), lc=lc, tm=tm, hid_chunk=hid_chunk),
        grid=(b, l // tm),
        in_specs=[row(h), pl.BlockSpec((1, 2, 6, d), lambda bi, j: (bi, 0, 0, 0)), _const_spec((1, d))]
                 + [row(x) for x in xs] + [_const_spec(w.shape) for w in wps]
                 + [_const_spec(w1.shape), _const_spec(w3.shape), _const_spec(w2.shape)],
        out_specs=row(h),
        out_shape=jax.ShapeDtypeStruct(h.shape, F32),
        compiler_params=_params("parallel", "parallel"),
        name="proj_ffn",
    )(h, mod, gn.reshape(1, d), *xs, *wps, w1, w3, w2)


def _mla_prep_kernel(h_ref, mod_ref, gn_ref, win_ref, qn_ref, kvn_ref, wq_ref, wk_ref, wv_ref,
                     qg_ref, kg_ref, ones_ref, cos_ref, sin_ref,
                     q_ref, k_ref, v_ref, zc_ref, *, lc, tm, q_rank, kv_rank, scale):
    sel = _row_select(mod_ref, pl.program_id(1) * tm, tm, lc)
    a = _norm_mod(h_ref[0], gn_ref[...], sel(1), sel(0))
    z = _bdot(a, win_ref[...])
    rms = lambda t, g: t * lax.rsqrt(jnp.mean(t * t, axis=-1, keepdims=True) + EPS) * g
    zq = rms(z[:, :q_rank], qn_ref[...]).astype(BF16)
    zkv = rms(z[:, q_rank:q_rank + kv_rank], kvn_ref[...]).astype(BF16)
    slots = wq_ref.shape[1]
    k0 = q_rank + kv_rank
    q = _bdot(zq, wq_ref[...])
    k = _bdot(zkv, wk_ref[...]) + z[:, k0:k0 + slots]
    v_ref[0] = _bdot(zkv, wv_ref[...]).astype(BF16)
    zc_ref[0] = z[:, k0 + slots:]

    lane = lax.broadcasted_iota(jnp.int32, (1, slots), 1) % LANES
    first_half = ((lane - MLA_NOPE) % AXIS_DIM) < AXIS_DIM // 2
    cos = cos_ref[...]
    sin = sin_ref[...]

    def head_norm_rope(t, gain):
        ss = _bdot(t * t, ones_ref[...]) * (1.0 / MLA_QK)
        t = t * lax.rsqrt(ss + EPS) * gain
        partner = jnp.where(first_half, pltpu.roll(t, slots - AXIS_DIM // 2, axis=1),
                            pltpu.roll(t, AXIS_DIM // 2, axis=1))
        return t * cos + partner * sin

    q_ref[0] = (head_norm_rope(q, qg_ref[...]) * scale).astype(BF16)
    k_ref[0] = head_norm_rope(k, kg_ref[...]).astype(BF16)


def _head_slots(w, width):
    kdim = w.shape[0]
    w = w.reshape(kdim, MLA_HEADS, width)
    return jnp.pad(w, ((0, 0), (0, 0), (0, LANES - width))).reshape(kdim, MLA_HEADS * LANES)


def _rope_slot_tables(lc, ll):
    rows = ll // GRID_W
    row = jnp.repeat(jnp.arange(rows, dtype=F32), GRID_W)
    col = jnp.tile(jnp.arange(GRID_W, dtype=F32), rows)
    inv = ROPE_THETA ** (-jnp.arange(0, AXIS_DIM, 2, dtype=F32) / AXIS_DIM)
    cos_parts, sin_parts = [], []
    for pos in (row, col):
        ang = pos[:, None] * inv
        cos_parts += [jnp.cos(ang), jnp.cos(ang)]
        sin_parts += [-jnp.sin(ang), jnp.sin(ang)]
    ones = jnp.ones((ll, MLA_NOPE), F32)
    zeros = jnp.zeros((ll, MLA_NOPE), F32)
    tail1 = jnp.ones((ll, LANES - MLA_QK), F32)
    tail0 = jnp.zeros((ll, LANES - MLA_QK), F32)
    cos = jnp.concatenate([ones] + cos_parts + [tail1], axis=1)
    sin = jnp.concatenate([zeros] + sin_parts + [tail0], axis=1)
    cos = jnp.concatenate([jnp.ones((lc, LANES), F32), cos], axis=0)
    sin = jnp.concatenate([jnp.zeros((lc, LANES), F32), sin], axis=0)
    return jnp.tile(cos, (1, MLA_HEADS)), jnp.tile(sin, (1, MLA_HEADS))


def mla_prep(h, mod, gn, p, tables, lc):
    b, l, d = h.shape
    q_rank = p['q_norm'].shape[0]
    kv_rank = p['kv_norm'].shape[0]
    slots = MLA_HEADS * LANES
    w_in = p['w_in']
    k0 = q_rank + kv_rank
    w_rope = jnp.tile(jnp.pad(w_in[:, k0:k0 + MLA_ROPE], ((0, 0), (MLA_NOPE, LANES - MLA_QK))), (1, MLA_HEADS))
    w_ext = jnp.concatenate([w_in[:, :k0], w_rope, w_in[:, k0 + MLA_ROPE:]], axis=1).astype(BF16)
    n_conv = w_in.shape[1] - k0 - MLA_ROPE
    wkv = p['wkv_b'].reshape(kv_rank, MLA_HEADS, MLA_NOPE + MLA_V)
    wq = _head_slots(p['wq_b'], MLA_QK).astype(BF16)
    wk = _head_slots(wkv[:, :, :MLA_NOPE].reshape(kv_rank, -1), MLA_NOPE).astype(BF16)
    wv = wkv[:, :, MLA_NOPE:].reshape(kv_rank, MLA_HEADS * MLA_V).astype(BF16)
    slot_gain = lambda g: jnp.tile(jnp.pad(g, (0, LANES - MLA_QK)), MLA_HEADS).reshape(1, slots)
    sid = jnp.arange(slots) // LANES
    ones = (sid[:, None] == sid[None, :]).astype(BF16)
    tm = _pick(l, (384, 256, 128, 64, 32, 16, 8))
    cos, sin = tables
    row = lambda n: pl.BlockSpec((1, tm, n), lambda bi, j: (bi, j, 0))
    tab = pl.BlockSpec((tm, slots), lambda bi, j: (j, 0))
    consts = [gn.reshape(1, d), w_ext, p['q_norm'].reshape(1, -1), p['kv_norm'].reshape(1, -1), wq, wk, wv,
              slot_gain(p['q_qk']), slot_gain(p['k_qk']), ones]
    return pl.pallas_call(
        functools.partial(_mla_prep_kernel, lc=lc, tm=tm, q_rank=q_rank, kv_rank=kv_rank, scale=MLA_QK ** -0.5),
        grid=(b, l // tm),
        in_specs=[row(d), pl.BlockSpec((1, 2, 6, d), lambda bi, j: (bi, 0, 0, 0))]
                 + [_const_spec(t.shape) for t in consts] + [tab, tab],
        out_specs=[row(slots), row(slots), row(MLA_HEADS * MLA_V), row(n_conv)],
        out_shape=[jax.ShapeDtypeStruct((b, l, slots), BF16), jax.ShapeDtypeStruct((b, l, slots), BF16),
                   jax.ShapeDtypeStruct((b, l, MLA_HEADS * MLA_V), BF16),
                   jax.ShapeDtypeStruct((b, l, n_conv), F32)],
        compiler_params=_params("parallel", "parallel"),
        name="mla_prep",
    )(h, mod, *consts, cos, sin)


def _attn_kernel(q_ref, k_ref, v_ref, o_ref):
    lane = lax.broadcasted_iota(jnp.int32, (1, LANES), 1)
    for hp in range(MLA_HEADS // 2):
        vp = v_ref[0, :, hp * LANES:(hp + 1) * LANES]
        outs = []
        for hh in range(2):
            hs = slice((2 * hp + hh) * LANES, (2 * hp + hh + 1) * LANES)
            s = lax.dot_general(q_ref[0, :, hs], k_ref[0, :, hs], NT_DIMS, preferred_element_type=F32)
            p = jnp.exp(s - jnp.max(s, axis=-1, keepdims=True))
            l = jnp.sum(p, axis=-1, keepdims=True)
            outs.append(jnp.dot(p.astype(BF16), vp, preferred_element_type=F32) / l)
        o_ref[0, :, hp * LANES:(hp + 1) * LANES] = jnp.where(lane < MLA_V, outs[0], outs[1]).astype(BF16)


def attention(q, k, v, klen):
    b, lq, slots = q.shape
    dv = v.shape[-1]
    tq = _pick(lq, (512, 256, 128, 64, 32, 16, 8))
    return pl.pallas_call(
        _attn_kernel,
        grid=(b, lq // tq),
        in_specs=[pl.BlockSpec((1, tq, slots), lambda bi, i: (bi, i, 0)),
                  pl.BlockSpec((1, klen, slots), lambda bi, i: (bi, 0, 0)),
                  pl.BlockSpec((1, klen, dv), lambda bi, i: (bi, 0, 0))],
        out_specs=pl.BlockSpec((1, tq, dv), lambda bi, i: (bi, i, 0)),
        out_shape=jax.ShapeDtypeStruct((b, lq, dv), BF16),
        compiler_params=_params("parallel", "parallel"),
        name="attention",
    )(q, k, v)


CONV_ROWS = 64
CONV_HALO = 16


def _conv_kernel(u_ref, w_ref, b_ref, g_ref, beta_ref, o_ref, pad_ref, y_ref, *, segments, ch):
    half = CONV_WIDTH // 2
    win = CONV_ROWS + 2 * CONV_HALO
    nlb = ch // LANES
    for seg_start, seg_len in segments:
        zeros = jnp.zeros((CONV_HALO, ch), F32)
        pad_ref[0:CONV_HALO, :] = zeros
        pad_ref[CONV_HALO + seg_len:2 * CONV_HALO + seg_len, :] = zeros

        def glu_body(i, carry):
            r0 = pl.multiple_of(i * CONV_ROWS, CONV_ROWS)
            u = u_ref[0, pl.ds(seg_start + r0, CONV_ROWS), :]
            pad_ref[pl.ds(CONV_HALO + r0, CONV_ROWS), :] = u[:, :ch] * _sigmoid(u[:, ch:])
            return carry

        lax.fori_loop(0, seg_len // CONV_ROWS, glu_body, 0)

        def conv_body(i, carry):
            r0 = pl.multiple_of(i * CONV_ROWS, CONV_ROWS)
            for lb in range(nlb):
                ls = slice(lb * LANES, (lb + 1) * LANES)
                xwin = pad_ref[pl.ds(r0, win), ls]
                acc = jnp.zeros((CONV_ROWS, LANES), F32)
                for j in range(CONV_WIDTH):
                    off = CONV_HALO - half + j
                    shifted = pltpu.roll(xwin, shift=(win - off) % win, axis=0)[0:CONV_ROWS]
                    acc = acc + shifted * w_ref[j:j + 1, ls]
                y_ref[pl.ds(seg_start + r0, CONV_ROWS), ls] = acc + b_ref[0:1, ls]
            return carry

        lax.fori_loop(0, seg_len // CONV_ROWS, conv_body, 0)

    def ln_body(i, carry):
        r0 = pl.multiple_of(i * CONV_ROWS, CONV_ROWS)
        y = y_ref[pl.ds(r0, CONV_ROWS), :]
        mu = jnp.mean(y, axis=-1, keepdims=True)
        d = y - mu
        var = jnp.mean(d * d, axis=-1, keepdims=True)
        z = d * lax.rsqrt(var + CONV_LN_EPS) * g_ref[0:1, :] + beta_ref[0:1, :]
        o_ref[0, pl.ds(r0, CONV_ROWS), :] = (z * _sigmoid(z)).astype(o_ref.dtype)
        return carry

    total = sum(s[1] for s in segments)
    lax.fori_loop(0, total // CONV_ROWS, ln_body, 0)


def conformer_conv(u, conv_w, conv_b, ln_g, ln_b, segments):
    b, l, ch2 = u.shape
    ch = ch2 // 2
    max_seg = max(s[1] for s in segments)
    row = lambda t: t.reshape(1, ch)
    return pl.pallas_call(
        functools.partial(_conv_kernel, segments=segments, ch=ch),
        grid=(b,),
        in_specs=[pl.BlockSpec((1, l, ch2), lambda i: (i, 0, 0)),
                  pl.BlockSpec((CONV_WIDTH, ch), lambda i: (0, 0)),
                  pl.BlockSpec((1, ch), lambda i: (0, 0)),
                  pl.BlockSpec((1, ch), lambda i: (0, 0)),
                  pl.BlockSpec((1, ch), lambda i: (0, 0))],
        out_specs=pl.BlockSpec((1, l, ch), lambda i: (i, 0, 0)),
        out_shape=jax.ShapeDtypeStruct((b, l, ch), BF16),
        scratch_shapes=[pltpu.VMEM((max_seg + 2 * CONV_HALO, ch), F32),
                        pltpu.VMEM((l, ch), F32)],
        compiler_params=_params("parallel"),
        name="conformer_conv",
    )(u, conv_w, row(conv_b), row(ln_g), row(ln_b))


def _rw_feat_kernel(*refs, lc, l, tm, has_vres):
    (h_ref, hp_ref, hn_ref, mod_ref, gn_ref, mu_ref, wr_ref, wk_ref, wv_ref, w1_ref, w2_ref, a1_ref, a2_ref,
     g1_ref, g2_ref, w0_ref, a0_ref, kk_gain_ref, ka_ref, ones_ref) = refs[:20]
    rest = refs[20:]
    if has_vres:
        vf_ref, v0_ref, v1_ref, v2_ref = rest[:4]
        rest = rest[4:]
    dir_ref, sh_ref, g_ref = rest
    d = h_ref.shape[-1]
    first = pl.program_id(1) * tm
    n_ext = tm + 2 * SUBLANES
    sel = _row_select(mod_ref, first - SUBLANES, n_ext, lc)
    h_ext = jnp.concatenate([hp_ref[0], h_ref[0], hn_ref[0]], axis=0)
    a_ext = _norm_mod(h_ext, gn_ref[...], sel(1), sel(0))
    cur = a_ext[SUBLANES:SUBLANES + tm]
    pos = first + lax.broadcasted_iota(jnp.int32, (tm, 1), 0)
    prev = pltpu.roll(a_ext, 1, axis=0)[SUBLANES:SUBLANES + tm]
    nxt = pltpu.roll(a_ext, n_ext - 1, axis=0)[SUBLANES:SUBLANES + tm]
    d_prev = jnp.where((pos == 0) | (pos == lc), 0.0, prev) - cur
    d_next = jnp.where((pos == lc - 1) | (pos == l - 1), 0.0, nxt) - cur
    mix = lambda i: (cur + d_prev * mu_ref[0, i:i + 1, :] + d_next * mu_ref[1, i:i + 1, :]).astype(BF16)

    xv = mix(3)
    k = _bdot(mix(2), wk_ref[...])
    v = _bdot(xv, wv_ref[...])
    if has_vres:
        gate = _sigmoid(v0_ref[...] + _bdot(_bdot(xv, v1_ref[...]), v2_ref[...]))
        v = v + (vf_ref[0] - v) * gate
    sh_ref[0, :, d:2 * d] = v
    sh_ref[0, :, 0:d] = _bdot(mix(0), wr_ref[...])
    g_ref[0] = _bdot(_sigmoid(_bdot(mix(5), g1_ref[...])), g2_ref[...])

    kkf = k * kk_gain_ref[...]
    ss = _bdot(kkf * kkf, ones_ref[...])
    kk = kkf * lax.rsqrt(jnp.maximum(ss, 1e-24))
    sh_ref[0, :, 2 * d:3 * d] = kk
    w_lora = _bdot(jnp.tanh(_bdot(mix(1), w1_ref[...])), w2_ref[...])
    a_lora = _bdot(_bdot(mix(4), a1_ref[...]), a2_ref[...])
    for di in range(2):
        dir_ref[di, 0, :, 0:d] = DECAY_SCALE * _sigmoid(w0_ref[di:di + 1, :] + w_lora[:, di * d:(di + 1) * d])
        a_rate = _sigmoid(a0_ref[di:di + 1, :] + a_lora[:, di * d:(di + 1) * d])
        dir_ref[di, 0, :, d:2 * d] = k * (1 + (a_rate - 1) * ka_ref[...])
        dir_ref[di, 0, :, 2 * d:3 * d] = kk * a_rate


def _pad_to(w, rows, cols):
    return jnp.pad(w, ((0, rows - w.shape[0]), (0, cols - w.shape[1])))


def _head_ones(d):
    hid = jnp.arange(d) // RW_HEAD
    return (hid[:, None] == hid[None, :]).astype(BF16)


def rw_features(h, mod, gn, p, v_first, lc):
    b, l, d = h.shape
    tm = _pick(lc, (256, 128, 64, 32, 16, 8))
    nblk = l // SUBLANES
    tb = tm // SUBLANES
    bf = lambda t: t.astype(BF16)
    cat2 = lambda t: jnp.concatenate([t[0], t[1]], axis=1)
    bdiag = lambda t: jnp.concatenate(
        [jnp.concatenate([t[0], jnp.zeros_like(t[0])], axis=1),
         jnp.concatenate([jnp.zeros_like(t[1]), t[1]], axis=1)], axis=0)
    lane_pad = lambda n: -(-n // LANES) * LANES
    rg = lane_pad(p['g1'].shape[1])
    consts = [gn.reshape(1, d), p['mu'], bf(p['wr']), bf(p['wk']), bf(p['wv']),
              bf(cat2(p['w1'])), bf(bdiag(p['w2'])), bf(cat2(p['a1'])), bf(bdiag(p['a2'])),
              bf(_pad_to(p['g1'], d, rg)), bf(_pad_to(p['g2'], rg, d)),
              p['w0'], p['a0'], p['k_k'].reshape(1, d), p['k_a'].reshape(1, d), _head_ones(d)]
    row = pl.BlockSpec((1, tm, d), lambda bi, j: (bi, j, 0))
    halo_prev = pl.BlockSpec((1, SUBLANES, d), lambda bi, j: (bi, jnp.maximum(j * tb - 1, 0), 0))
    halo_next = pl.BlockSpec((1, SUBLANES, d), lambda bi, j: (bi, jnp.minimum((j + 1) * tb, nblk - 1), 0))
    in_specs = [row, halo_prev, halo_next, pl.BlockSpec((1, 2, 6, d), lambda bi, j: (bi, 0, 0, 0))]
    in_specs += [_const_spec(t.shape) for t in consts]
    args = [h, h, h, mod] + consts
    has_vres = v_first is not None
    if has_vres:
        rv = lane_pad(p['v1'].shape[1])
        extra = [p['v0'].reshape(1, d), bf(_pad_to(p['v1'], d, rv)), bf(_pad_to(p['v2'], rv, d))]
        in_specs += [pl.BlockSpec((1, tm, d), lambda bi, j: (bi, j, 1))] + [_const_spec(t.shape) for t in extra]
        args += [v_first] + extra
    return pl.pallas_call(
        functools.partial(_rw_feat_kernel, lc=lc, l=l, tm=tm, has_vres=has_vres),
        grid=(b, l // tm),
        in_specs=in_specs,
        out_specs=[pl.BlockSpec((2, 1, tm, 3 * d), lambda bi, j: (0, bi, j, 0)),
                   pl.BlockSpec((1, tm, 3 * d), lambda bi, j: (bi, j, 0)), row],
        out_shape=[jax.ShapeDtypeStruct((2, b, l, 3 * d), F32), jax.ShapeDtypeStruct((b, l, 3 * d), F32),
                   jax.ShapeDtypeStruct((b, l, d), F32)],
        compiler_params=_params("parallel", "parallel"),
        name="rw_features",
    )(*args)


def _tile_rows(x, mask):
    return jnp.concatenate([x.astype(BF16)] * SCAN_G, axis=0) * mask


def _scan_kernel(dirf_ref, shf_ref, dirr_ref, shr_ref, cum_ref, strict_ref, incl_ref, lvl_ref, blk_ref,
                 yf_ref, yr_ref, state_ref):
    c = SCAN_C
    @pl.when(pl.program_id(1) == 0)
    def _():
        state_ref[...] = jnp.zeros_like(state_ref)

    blk = blk_ref[...]
    d = yf_ref.shape[-1]
    dir_refs = (dirf_ref, dirr_ref)
    sh_refs = (shf_ref, shr_ref)
    y_refs = (yf_ref, yr_ref)
    chains = [(di, g) for di in range(2) for g in range(d // SCAN_W)]
    ls = lambda g, part=0: slice(part * d + g * SCAN_W, part * d + (g + 1) * SCAN_W)
    bf = lambda t: t.astype(BF16)
    gc = SCAN_G * c

    nsub = dirf_ref.shape[2] // c
    for sub in range(nsub):
        rows = (slice(sub * c, (sub + 1) * c), slice((nsub - 1 - sub) * c, (nsub - sub) * c))
        lw = [dir_refs[di][0, 0, rows[di], ls(g)] for di, g in chains]
        lp = []
        for n, (di, g) in enumerate(chains):
            hi = bf(lw[n])
            lo = bf(lw[n] - hi.astype(F32))
            lp.append(_bdot(cum_ref[di], jnp.concatenate([hi, lo], axis=0)))

        total, ar, bk, bk_tail = [], [], [], []
        for n, (di, g) in enumerate(chains):
            tot = jnp.sum(lw[n], axis=0, keepdims=True)
            inv = jnp.exp(-lp[n])
            tail = jnp.exp(tot - lp[n])
            kd = dir_refs[di][0, 0, rows[di], ls(g, 1)]
            bd = dir_refs[di][0, 0, rows[di], ls(g, 2)]
            at = -sh_refs[di][0, rows[di], ls(g, 2)] * jnp.exp(lp[n] - lw[n])
            rt = sh_refs[di][0, rows[di], ls(g, 0)] * jnp.exp(lp[n])
            total.append(tot)
            ar.append(bf(jnp.concatenate([at, rt], axis=0)))
            bk.append(jnp.concatenate([_tile_rows(bd * inv, blk), _tile_rows(kd * inv, blk)], axis=0))
            bk_tail.append(bf(jnp.concatenate([bd * tail, kd * tail], axis=0)))

        gg = [_bdot(ar[n], bk[n], NT_DIMS) for n in range(len(chains))]
        sh = [_bdot(ar[n], bf(state_ref[di, g]), NT_DIMS) for n, (di, g) in enumerate(chains)]

        vext = [_tile_rows(sh_refs[di][0, rows[di], ls(g, 1)], blk) for di, g in chains]
        rhs = [sh[n][:c] + _bdot(bf(gg[n][:c, gc:] * strict_ref[di]), vext[n]) for n, (di, g) in enumerate(chains)]

        m = [gg[n][:c, :gc] * strict_ref[di] for n, (di, g) in enumerate(chains)]
        for lv in range(lvl_ref.shape[1]):
            m = [m[n] + _bdot(bf(m[n] * lvl_ref[di, lv]), _tile_rows(m[n], blk)) for n, (di, g) in enumerate(chains)]
        u = [rhs[n] + _bdot(bf(m[n]), _tile_rows(rhs[n], blk)) for n in range(len(chains))]

        for n, (di, g) in enumerate(chains):
            incl = incl_ref[di]
            rbk = bf(gg[n][c:] * jnp.concatenate([incl, incl], axis=1))
            y = sh[n][c:] + _bdot(rbk, jnp.concatenate([_tile_rows(u[n], blk), vext[n]], axis=0))
            y_refs[di][0, rows[di], ls(g)] = y
        for n, (di, g) in enumerate(chains):
            uv = bf(jnp.concatenate([u[n], sh_refs[di][0, rows[di], ls(g, 1)]], axis=0))
            upd = _bdot(uv, bk_tail[n], TN_DIMS)
            state_ref[di, g] = (state_ref[di, g] * jnp.exp(total[n]) + upd) * blk.astype(F32)


def _scan_masks():
    c, g = SCAN_C, SCAN_G
    t = jnp.arange(c)[:, None]
    s = jnp.arange(c)[None, :]
    cum = jnp.stack([s <= t, s >= t])
    cum = jnp.concatenate([cum, cum], axis=2).astype(BF16)
    sg = jnp.arange(g * c)[None, :] % c
    strict = jnp.stack([sg < t, sg > t]).astype(F32)
    incl = jnp.stack([sg <= t, sg >= t]).astype(F32)
    sizes = [1 << i for i in range(c.bit_length() - 1)]
    lvl = jnp.stack([((t // (2 * s) == sg // (2 * s)) & (t // s != sg // s)).astype(F32) for s in sizes])
    lvl = strict[:, None] * lvl[None]
    rb = jnp.arange(SCAN_W)[:, None] // RW_HEAD
    cb = jnp.arange(SCAN_W)[None, :] // RW_HEAD
    blk = (rb == cb).astype(BF16)
    return cum, strict, incl, lvl, blk


def wkv_scan(dirs, shared, lc):
    b, l, d3 = shared.shape
    d = d3 // 3
    c = SCAN_C * SCAN_SUB
    nc = l // c
    ncc = lc // c
    masks = _scan_masks()

    def rev(ci):
        return jnp.where(ci < ncc, ncc - 1 - ci, nc - 1 + ncc - ci)

    fwd2 = pl.BlockSpec((1, 1, c, d3), lambda bi, ci: (0, bi, ci, 0))
    rev2 = pl.BlockSpec((1, 1, c, d3), lambda bi, ci: (1, bi, rev(ci), 0))
    fwd1 = pl.BlockSpec((1, c, d3), lambda bi, ci: (bi, ci, 0))
    rev1 = pl.BlockSpec((1, c, d3), lambda bi, ci: (bi, rev(ci), 0))
    out = jax.ShapeDtypeStruct((b, l, d), F32)
    return pl.pallas_call(
        _scan_kernel,
        grid=(b, nc),
        in_specs=[fwd2, fwd1, rev2, rev1] + [_const_spec(t.shape) for t in masks],
        out_specs=[pl.BlockSpec((1, c, d), lambda bi, ci: (bi, ci, 0)),
                   pl.BlockSpec((1, c, d), lambda bi, ci: (bi, rev(ci), 0))],
        out_shape=[out, out],
        scratch_shapes=[pltpu.VMEM((2, d // SCAN_W, SCAN_W, SCAN_W), F32)],
        compiler_params=_params("parallel", "arbitrary"),
        name="wkv_scan",
    )(dirs, shared, dirs, shared, *masks)


def _head_sums(x, gather_ref, spread_ref, exact):
    dot = _dot2 if exact else _bdot
    return _dot2(dot(x, gather_ref[...]), spread_ref[...])


def _rw_readout_kernel(yf_ref, yr_ref, r_ref, v_ref, g_ref, kd_ref, rk_ref, lng_ref, lnb_ref, gather_ref, spread_ref,
                       o_ref):
    inv_n = 1.0 / RW_HEAD
    y = yf_ref[0] + yr_ref[0]
    mu = _head_sums(y, gather_ref, spread_ref, True) * inv_n
    dy = y - mu
    var = _head_sums(dy * dy, gather_ref, spread_ref, False) * inv_n
    yn = dy * lax.rsqrt(var + RW_GN_EPS) * lng_ref[...] + lnb_ref[...]
    kmix = kd_ref[0, 0] * rk_ref[0:1, :] + kd_ref[1, 0] * rk_ref[1:2, :]
    coef = _head_sums(r_ref[0] * kmix, gather_ref, spread_ref, True)
    o_ref[0] = ((yn + coef * v_ref[0]) * g_ref[0]).astype(BF16)


def rw_readout(yf, yr, shared, g, dirs, p):
    b, l, d = g.shape
    tm = _pick(l, (384, 256, 128, 64, 32, 16, 8))
    row = pl.BlockSpec((1, tm, d), lambda bi, j: (bi, j, 0))
    v_row = pl.BlockSpec((1, tm, d), lambda bi, j: (bi, j, 1))
    kd_rows = pl.BlockSpec((2, 1, tm, d), lambda bi, j: (0, bi, j, 1))
    gather = (jnp.arange(d)[:, None] // RW_HEAD == jnp.arange(LANES)[None, :]).astype(BF16)
    consts = [p['r_k'].reshape(2, d), p['ln_g'].reshape(1, d), p['ln_b'].reshape(1, d), gather, gather.T]
    return pl.pallas_call(
        _rw_readout_kernel,
        grid=(b, l // tm),
        in_specs=[row, row, row, v_row, row, kd_rows] + [_const_spec(t.shape) for t in consts],
        out_specs=row,
        out_shape=jax.ShapeDtypeStruct((b, l, d), BF16),
        compiler_params=_params("parallel", "parallel"),
        name="rw_readout",
    )(yf, yr, shared, shared, g, dirs, *consts)


def kernel(x, c, ctx, c_ctx, ada_w, ada_b, norm_mix, norm_ffn, ffn_w1, ffn_w3, ffn_w2, even_w_in, mla_q_norm, mla_wq_b, mla_kv_norm, mla_wkv_b, mla_q_qk, mla_k_qk, conv_w, conv_b, conv_ln_g, conv_ln_b, even_w_out, rw_mu, rw_wr, rw_wk, rw_wv, rw_w0, rw_w1, rw_w2, rw_a0, rw_a1, rw_a2, rw_v0, rw_v1, rw_v2, rw_k_k, rw_k_a, rw_r_k, rw_g1, rw_g2, rw_ln_g, rw_ln_b, rw_wo):
    b, ll, d = x.shape
    lc = ctx.shape[1]
    depth = ada_w.shape[0]
    bf = lambda t: t.astype(BF16)
    h = jnp.concatenate([ctx, x], axis=1)
    tables = _rope_slot_tables(lc, ll)

    cond = jnp.concatenate([c, c_ctx[None, :]], axis=0)
    cond = jnp.pad(cond, ((0, -(-(b + 1) // SUBLANES) * SUBLANES - (b + 1)), (0, 0)))
    v_first = None
    for i in range(depth):
        j = i // 2
        m = mm(cond, ada_w[i], pre="silu")[:b + 1] + ada_b[i]
        m_lat = m[:b].reshape(b, 1, 6, d)
        m_ctx = jnp.broadcast_to(m[b].reshape(1, 1, 6, d), (b, 1, 6, d))
        mod = jnp.concatenate([m_ctx, m_lat], axis=1)
        if i % 2 == 0:
            p = {'w_in': even_w_in[j], 'q_norm': mla_q_norm[j], 'wq_b': mla_wq_b[j],
                 'kv_norm': mla_kv_norm[j], 'wkv_b': mla_wkv_b[j], 'q_qk': mla_q_qk[j], 'k_qk': mla_k_qk[j]}
            q, k, v, zconv = mla_prep(h, mod, norm_mix[i], p, tables, lc)
            o = jnp.concatenate([attention(q[:, :lc], k, v, lc), attention(q[:, lc:], k, v, lc + ll)], axis=1)
            cv = conformer_conv(zconv, conv_w[j], conv_b[j], conv_ln_g[j], conv_ln_b[j], ((0, lc), (lc, ll)))
            n_attn = o.shape[-1]
            xs = [o, cv]
            wps = [bf(even_w_out[j][:n_attn]), bf(even_w_out[j][n_attn:])]
        else:
            p = {'mu': rw_mu[j], 'wr': rw_wr[j], 'wk': rw_wk[j], 'wv': rw_wv[j],
                 'w0': rw_w0[j], 'w1': rw_w1[j], 'w2': rw_w2[j],
                 'a0': rw_a0[j], 'a1': rw_a1[j], 'a2': rw_a2[j],
                 'k_k': rw_k_k[j], 'k_a': rw_k_a[j], 'r_k': rw_r_k[j],
                 'g1': rw_g1[j], 'g2': rw_g2[j], 'ln_g': rw_ln_g[j], 'ln_b': rw_ln_b[j]}
            if j > 0:
                p['v0'] = rw_v0[j - 1]
                p['v1'] = rw_v1[j - 1]
                p['v2'] = rw_v2[j - 1]
            dirs, shared, g = rw_features(h, mod, norm_mix[i], p, v_first if j > 0 else None, lc)
            if v_first is None:
                v_first = shared
            yf, yr = wkv_scan(dirs, shared, lc)
            xs = [rw_readout(yf, yr, shared, g, dirs, p)]
            wps = [bf(rw_wo[j])]
        h = proj_ffn(h, xs, wps, mod, norm_ffn[i], bf(ffn_w1[i]), bf(ffn_w3[i]), bf(ffn_w2[i]), lc)
    return h[:, lc:]
```
